```python
import math
import jax, jax.numpy as jnp
from jax import lax
import numpy as np

D_MODEL = 4096
BATCH = 4
SEQ = 2048
DEPTH = 4
DEC_BATCH = 8
DEC_SEQ = 4
PAST_LEN = 8192
PAGE_SIZE = 128

N_A_LAYERS = DEPTH // 2
D_INNER = 2 * D_MODEL
SSM_HEAD_DIM = 64
N_SSM_HEADS = D_INNER // SSM_HEAD_DIM
D_STATE = 128
N_SSM_GROUPS = 8
SSM_CONV = 4
SSD_CHUNK = 128
CONV_DIM = D_INNER + 2 * N_SSM_GROUPS * D_STATE
IN_DIM = D_INNER + CONV_DIM + N_SSM_HEADS
HEAD_DIM = 128
N_SLOTS = D_MODEL // HEAD_DIM
N_KV_HEADS = 8
KV_REP = N_SLOTS // N_KV_HEADS
DIL_WINDOWS = (128, 512, 2048)
DIL_RATES = (1, 4, 16)
N_DIL_GROUPS = len(DIL_WINDOWS)
WINDOW_MAX = max(DIL_WINDOWS)
Q_BLOCK = 128
N_BUCKETS = 32
REL_MAX_DIST = WINDOW_MAX
D_FF = 256 * ((8 * D_MODEL // 3 + 255) // 256)
FFN_CONV = 3
PLE_DIM = 256
NORM_EPS = 1e-6

kernel_name = 'yoco_ssd_dilated_window_hybrid_step'


def _rms(x, g):
    xf = x.astype(jnp.float32)
    y = xf * lax.rsqrt(jnp.mean(xf * xf, axis=-1, keepdims=True) + NORM_EPS)
    return (y * g.astype(jnp.float32)).astype(x.dtype)


def _causal_dwconv(u, prev, w, b):
    t, width = u.shape[1], w.shape[0]
    up = jnp.concatenate([prev.astype(u.dtype), u], axis=1)
    y = up[:, 0:t] * w[0] + b
    for k in range(1, width):
        y = y + up[:, k:k + t] * w[k]
    return y, up[:, up.shape[1] - (width - 1):]


def _rel_bucket(dist):
    max_exact = N_BUCKETS // 2
    n = np.asarray(dist, dtype=np.int64)
    ratio = np.log(np.maximum(n, 1) / max_exact) / np.log(REL_MAX_DIST / max_exact)
    large = np.minimum(max_exact + (np.maximum(ratio, 0.0) * (N_BUCKETS - max_exact)).astype(np.int64), N_BUCKETS - 1)
    return np.where(n < max_exact, n, large).astype(np.int32)


def _dists(g):
    return (np.arange(DIL_WINDOWS[g] // DIL_RATES[g] + 1) * DIL_RATES[g]).astype(np.int32)


def _ssd(x, dt, a, bm, cm, h0):
    f32 = jnp.float32
    b, t, nh, hp = x.shape
    ng, ns = bm.shape[2], bm.shape[3]
    nr = nh // ng
    L = min(SSD_CHUNK, t)
    pad = (-t) % L
    x, bm, cm = x.astype(f32), bm.astype(f32), cm.astype(f32)
    if pad:
        padt = lambda z: jnp.pad(z, [(0, 0), (0, pad)] + [(0, 0)] * (z.ndim - 2))
        x, dt, bm, cm = padt(x), padt(dt), padt(bm), padt(cm)
    nc = (t + pad) // L
    xr = x.reshape(b, nc, L, ng, nr, hp)
    dtr = dt.reshape(b, nc, L, ng, nr)
    br = bm.reshape(b, nc, L, ng, ns)
    cr = cm.reshape(b, nc, L, ng, ns)
    xdt = xr * dtr[..., None]
    ac = jnp.cumsum(jnp.moveaxis(dtr * a.reshape(ng, nr), 2, -1), axis=-1)
    causal = jnp.tril(jnp.ones((L, L), dtype=bool))
    decay_in = jnp.exp(jnp.where(causal, ac[..., :, None] - ac[..., None, :], -jnp.inf))
    cb = jnp.einsum('bclgn,bcsgn->bcgls', cr, br)
    y_diag = jnp.einsum('bcgls,bcgrls,bcsgrp->bclgrp', cb, decay_in, xdt)
    st = jnp.einsum('bclgn,bcgrl,bclgrp->bcgrpn', br, jnp.exp(ac[..., -1:] - ac), xdt)

    def step(h, inp):
        dec, s = inp
        return dec[..., None, None] * h + s, h

    h_last, h_prev = lax.scan(step, h0.astype(f32).reshape(b, ng, nr, hp, ns),
                              (jnp.moveaxis(jnp.exp(ac[..., -1]), 1, 0), jnp.moveaxis(st, 1, 0)))
    y_off = jnp.einsum('bclgn,cbgrpn,bcgrl->bclgrp', cr, h_prev, jnp.exp(ac))
    y = (y_diag + y_off).reshape(b, nc * L, nh, hp)[:, :t]
    return y, h_last.reshape(b, nh, hp, ns)


def _mamba2_mixer(h, h0, conv0, w_in, conv_w, conv_b, dt_bias, a_log, d_skip, norm_g, w_out):
    b, t, _ = h.shape
    gn = N_SSM_GROUPS * D_STATE
    zxbcdt = h @ w_in
    z = zxbcdt[..., :D_INNER]
    xbc, conv_new = _causal_dwconv(zxbcdt[..., D_INNER:D_INNER + CONV_DIM], conv0, conv_w, conv_b)
    xbc = jax.nn.silu(xbc)
    xs = xbc[..., :D_INNER].reshape(b, t, N_SSM_HEADS, SSM_HEAD_DIM)
    bm = xbc[..., D_INNER:D_INNER + gn].reshape(b, t, N_SSM_GROUPS, D_STATE)
    cm = xbc[..., D_INNER + gn:].reshape(b, t, N_SSM_GROUPS, D_STATE)
    dt = jax.nn.softplus(zxbcdt[..., D_INNER + CONV_DIM:].astype(jnp.float32) + dt_bias.astype(jnp.float32))
    a = -jnp.exp(a_log.astype(jnp.float32))
    y, h_new = _ssd(xs, dt, a, bm, cm, h0)
    y = y + xs.astype(jnp.float32) * d_skip.astype(jnp.float32)[:, None]
    y = y.reshape(b, t, D_INNER) * jax.nn.silu(z.astype(jnp.float32))
    y = _rms(y.reshape(b, t, N_SSM_GROUPS, D_INNER // N_SSM_GROUPS),
             norm_g.reshape(N_SSM_GROUPS, D_INNER // N_SSM_GROUPS)).reshape(b, t, D_INNER)
    return y.astype(h.dtype) @ w_out, h_new.astype(h0.dtype), conv_new


def _dilated_attention(q, k_ctx, v_ctx, start, rel_bias):
    b, t = q.shape[0], q.shape[1]
    dists, biases = [], []
    for g in range(N_DIL_GROUPS):
        d_np = _dists(g)
        dists.append(jnp.asarray(d_np))
        tab = rel_bias[_rel_bucket(d_np)][:, g * N_SLOTS:(g + 1) * N_SLOTS].astype(jnp.float32)
        biases.append(tab.T.reshape(N_KV_HEADS, KV_REP, -1))

    def block(qb, j0):
        jj = j0 + jnp.arange(qb.shape[1], dtype=jnp.int32)
        outs, lses = [], []
        for g in range(N_DIL_GROUPS):
            rel = jj[:, None] - dists[g][None, :]
            idx = WINDOW_MAX + rel
            kg = k_ctx[:, idx]
            vg = v_ctx[:, idx]
            s = jnp.einsum('bqhrd,bqkhd->bqhrk', qb[:, :, g], kg).astype(jnp.float32) + biases[g]
            valid = (start + rel) >= 0
            s = jnp.where(valid[None, :, None, None, :], s, -jnp.inf)
            lse = jax.nn.logsumexp(s, axis=-1)
            p = jnp.exp(s - lse[..., None])
            outs.append(jnp.einsum('bqhrk,bqkhd->bqhrd', p, vg))
            lses.append(lse)
        w = jax.nn.softmax(jnp.stack(lses, axis=-1), axis=-1)
        o = jnp.einsum('bqhrg,gbqhrd->bqhrd', w, jnp.stack(outs, axis=0))
        return o.astype(qb.dtype)

    if t > Q_BLOCK and t % Q_BLOCK == 0:
        nb = t // Q_BLOCK
        qblocks = jnp.moveaxis(q.reshape((b, nb, Q_BLOCK) + q.shape[2:]), 1, 0)
        j0s = jnp.arange(nb, dtype=jnp.int32) * Q_BLOCK
        o = lax.map(lambda args: block(args[0], args[1]), (qblocks, j0s))
        o = jnp.moveaxis(o, 0, 1).reshape((b, t) + o.shape[3:])
    else:
        o = block(q, jnp.int32(0))
    return o


def _trunk(x, p, start, ssm0, convs0, convf0, k_pre, v_pre, W):
    b, t, _ = x.shape
    ssm_out, convs_out, convf_out = [], [], []
    k = v = k_ctx = v_ctx = None
    for i in range(DEPTH):
        if i < N_A_LAYERS:
            h = _rms(x, W['norm_mix'][i])
            mix, hs, cs = _mamba2_mixer(h, ssm0[i], convs0[i], W['ssm_w_in'][i], W['ssm_conv_w'][i],
                                        W['ssm_conv_b'][i], W['ssm_dt_bias'][i], W['ssm_a_log'][i],
                                        W['ssm_d'][i], W['ssm_norm'][i], W['ssm_w_out'][i])
            ssm_out.append(hs)
            convs_out.append(cs)
        else:
            li = i - N_A_LAYERS
            if li == 0:
                hkv = _rms(x, W['norm_kv'])
                k = _rms((hkv @ W['w_k']).reshape(b, t, N_KV_HEADS, HEAD_DIM), W['k_norm'])
                v = (hkv @ W['w_v']).reshape(b, t, N_KV_HEADS, HEAD_DIM)
                k_ctx = jnp.concatenate([k_pre.astype(k.dtype), k], axis=1)
                v_ctx = jnp.concatenate([v_pre.astype(v.dtype), v], axis=1)
            h = _rms(x, W['norm_mix'][i])
            q = (h @ W['attn_w_q'][li]).reshape(b, t, N_DIL_GROUPS, N_SLOTS, HEAD_DIM)
            q = _rms(q, W['q_norm'][li][:, None, :]) * (HEAD_DIM ** -0.5)
            q = q.reshape(b, t, N_DIL_GROUPS, N_KV_HEADS, KV_REP, HEAD_DIM)
            o = _dilated_attention(q, k_ctx, v_ctx, start, W['rel_bias'])
            mix = o.reshape(b, t, N_SLOTS * HEAD_DIM) @ W['attn_w_o'][li]
        x = x + mix
        h = _rms(x, W['norm_ffn'][i])
        u, cf = _causal_dwconv(h @ W['ffn_w_up'][i], convf0[i], W['ffn_conv_w'][i], W['ffn_conv_b'][i])
        convf_out.append(cf)
        x = x + (jax.nn.silu(u[..., :D_FF]) * u[..., D_FF:]) @ W['ffn_w_down'][i]
        gate = jax.nn.sigmoid(_rms(x, W['norm_ple'][i]) @ W['ple_w_gate'][i])
        x = x + gate * (p[i] @ W['ple_w_proj'][i])
    return x, jnp.stack(ssm_out), jnp.stack(convs_out), jnp.stack(convf_out), k, v


def setup_inputs(seed: int = 0) -> dict:
    key = jax.random.key(seed)
    keys = iter(jax.random.split(key, 48))
    f32 = jnp.float32

    def nrm(shape, scale):
        return jax.random.normal(next(keys), shape, f32) * scale

    def gain(shape):
        return 1.0 + nrm(shape, 0.02)

    n_b = DEPTH - N_A_LAYERS
    wbuf = min(WINDOW_MAX, PAST_LEN)
    dt0 = jnp.exp(jax.random.uniform(next(keys), (N_A_LAYERS, N_SSM_HEADS), f32, math.log(1e-3), math.log(1e-1)))
    a_init = jax.random.uniform(next(keys), (N_A_LAYERS, N_SSM_HEADS), f32, 1.0, 16.0)
    return {
        'x_prompt': nrm((BATCH, SEQ, D_MODEL), 1.0),
        'x_sample': nrm((DEC_BATCH, DEC_SEQ, D_MODEL), 1.0),
        'p_prompt': nrm((DEPTH, BATCH, SEQ, PLE_DIM), 1.0),
        'p_sample': nrm((DEPTH, DEC_BATCH, DEC_SEQ, PLE_DIM), 1.0),
        'state_ssm': nrm((N_A_LAYERS, DEC_BATCH, N_SSM_HEADS, SSM_HEAD_DIM, D_STATE), 0.1),
        'state_conv_ssm': nrm((N_A_LAYERS, DEC_BATCH, SSM_CONV - 1, CONV_DIM), 1.0),
        'state_conv_ffn': nrm((DEPTH, DEC_BATCH, FFN_CONV - 1, 2 * D_FF), 1.0),
        'cache_k_window': nrm((DEC_BATCH, wbuf, N_KV_HEADS, HEAD_DIM), 1.0),
        'cache_v_window': nrm((DEC_BATCH, wbuf, N_KV_HEADS, HEAD_DIM), 1.0),
        'norm_mix': gain((DEPTH, D_MODEL)),
        'norm_ffn': gain((DEPTH, D_MODEL)),
        'norm_ple': gain((DEPTH, D_MODEL)),
        'ssm_w_in': nrm((N_A_LAYERS, D_MODEL, IN_DIM), D_MODEL ** -0.5),
        'ssm_conv_w': nrm((N_A_LAYERS, SSM_CONV, CONV_DIM), SSM_CONV ** -0.5),
        'ssm_conv_b': nrm((N_A_LAYERS, CONV_DIM), 0.02),
        'ssm_dt_bias': dt0 + jnp.log(-jnp.expm1(-dt0)),
        'ssm_a_log': jnp.log(a_init),
        'ssm_d': 1.0 + nrm((N_A_LAYERS, N_SSM_HEADS), 0.1),
        'ssm_norm': gain((N_A_LAYERS, D_INNER)),
        'ssm_w_out': nrm((N_A_LAYERS, D_INNER, D_MODEL), D_INNER ** -0.5),
        'norm_kv': gain((D_MODEL,)),
        'w_k': nrm((D_MODEL, N_KV_HEADS * HEAD_DIM), D_MODEL ** -0.5),
        'w_v': nrm((D_MODEL, N_KV_HEADS * HEAD_DIM), D_MODEL ** -0.5),
        'k_norm': gain((HEAD_DIM,)),
        'attn_w_q': nrm((n_b, D_MODEL, N_DIL_GROUPS * N_SLOTS * HEAD_DIM), D_MODEL ** -0.5),
        'q_norm': gain((n_b, N_DIL_GROUPS, HEAD_DIM)),
        'attn_w_o': nrm((n_b, N_SLOTS * HEAD_DIM, D_MODEL), (N_SLOTS * HEAD_DIM) ** -0.5),
        'rel_bias': nrm((N_BUCKETS, N_DIL_GROUPS * N_SLOTS), 0.5),
        'ffn_w_up': nrm((DEPTH, D_MODEL, 2 * D_FF), D_MODEL ** -0.5),
        'ffn_conv_w': nrm((DEPTH, FFN_CONV, 2 * D_FF), FFN_CONV ** -0.5),
        'ffn_conv_b': nrm((DEPTH, 2 * D_FF), 0.02),
        'ffn_w_down': nrm((DEPTH, D_FF, D_MODEL), D_FF ** -0.5),
        'ple_w_proj': nrm((DEPTH, PLE_DIM, D_MODEL), PLE_DIM ** -0.5),
        'ple_w_gate': nrm((DEPTH, D_MODEL, D_MODEL), D_MODEL ** -0.5),
    }


def reference(x_prompt, x_sample, p_prompt, p_sample, state_ssm, state_conv_ssm, state_conv_ffn,
              cache_k_window, cache_v_window, norm_mix, norm_ffn, norm_ple, ssm_w_in, ssm_conv_w,
              ssm_conv_b, ssm_dt_bias, ssm_a_log, ssm_d, ssm_norm, ssm_w_out, norm_kv, w_k, w_v, k_norm,
              attn_w_q, q_norm, attn_w_o, rel_bias, ffn_w_up, ffn_conv_w, ffn_conv_b, ffn_w_down,
              ple_w_proj, ple_w_gate):
    W = dict(norm_mix=norm_mix, norm_ffn=norm_ffn, norm_ple=norm_ple, ssm_w_in=ssm_w_in,
             ssm_conv_w=ssm_conv_w, ssm_conv_b=ssm_conv_b, ssm_dt_bias=ssm_dt_bias, ssm_a_log=ssm_a_log,
             ssm_d=ssm_d, ssm_norm=ssm_norm, ssm_w_out=ssm_w_out, norm_kv=norm_kv, w_k=w_k, w_v=w_v,
             k_norm=k_norm, attn_w_q=attn_w_q, q_norm=q_norm, attn_w_o=attn_w_o, rel_bias=rel_bias,
             ffn_w_up=ffn_w_up, ffn_conv_w=ffn_conv_w, ffn_conv_b=ffn_conv_b, ffn_w_down=ffn_w_down,
             ple_w_proj=ple_w_proj, ple_w_gate=ple_w_gate)
    bp, tp = x_prompt.shape[0], x_prompt.shape[1]
    sd = state_ssm.dtype
    ssm0_p = jnp.zeros((N_A_LAYERS, bp, N_SSM_HEADS, SSM_HEAD_DIM, D_STATE), sd)
    convs0_p = jnp.zeros((N_A_LAYERS, bp, SSM_CONV - 1, CONV_DIM), state_conv_ssm.dtype)
    convf0_p = jnp.zeros((DEPTH, bp, FFN_CONV - 1, 2 * D_FF), state_conv_ffn.dtype)
    kv0_p = jnp.zeros((bp, WINDOW_MAX, N_KV_HEADS, HEAD_DIM), cache_k_window.dtype)
    y_prompt, ssm_p, convs_p, convf_p, k_p, v_p = _trunk(x_prompt, p_prompt, 0, ssm0_p, convs0_p, convf0_p,
                                                         kv0_p, kv0_p, W)
    keep = min(WINDOW_MAX, tp)
    k_win_p = k_p[:, tp - keep:]
    v_win_p = v_p[:, tp - keep:]
    pad = WINDOW_MAX - cache_k_window.shape[1]
    k_pre_s = jnp.pad(cache_k_window, ((0, 0), (pad, 0), (0, 0), (0, 0)))
    v_pre_s = jnp.pad(cache_v_window, ((0, 0), (pad, 0), (0, 0), (0, 0)))
    y_sample, ssm_s, convs_s, convf_s, k_s, v_s = _trunk(x_sample, p_sample, PAST_LEN, state_ssm, state_conv_ssm,
                                                         state_conv_ffn, k_pre_s, v_pre_s, W)
    return (y_prompt, y_sample, ssm_p, ssm_s, convs_p, convs_s, convf_p, convf_s, k_win_p, v_win_p, k_s, v_s)
```

```python
import functools

import numpy as np
import jax
import jax.numpy as jnp
from jax import lax
from jax.experimental import pallas as pl
from jax.experimental.pallas import tpu as pltpu

F32 = jnp.float32
BF16 = jnp.bfloat16

D_MODEL = 4096
DEPTH = 4
N_A_LAYERS = DEPTH // 2
PAST_LEN = 8192
D_INNER = 2 * D_MODEL
SSM_HEAD_DIM = 64
N_SSM_HEADS = D_INNER // SSM_HEAD_DIM
D_STATE = 128
N_SSM_GROUPS = 8
HEADS_PER_GROUP = N_SSM_HEADS // N_SSM_GROUPS
GROUP_WIDTH = D_INNER // N_SSM_GROUPS
SSM_CONV = 4
SSD_CHUNK = 128
CONV_DIM = D_INNER + 2 * N_SSM_GROUPS * D_STATE
XBC_END = D_INNER + CONV_DIM
HEAD_DIM = 128
N_SLOTS = D_MODEL // HEAD_DIM
N_KV_HEADS = 8
KV_REP = N_SLOTS // N_KV_HEADS
DIL_WINDOWS = (128, 512, 2048)
DIL_RATES = (1, 4, 16)
N_DIL_GROUPS = len(DIL_WINDOWS)
WINDOW_MAX = max(DIL_WINDOWS)
Q_BLOCK = 128
N_BUCKETS = 32
REL_MAX_DIST = WINDOW_MAX
D_FF = 256 * ((8 * D_MODEL // 3 + 255) // 256)
FFN_CONV = 3
PLE_DIM = 256
NORM_EPS = 1e-6

LANES = 128
SUBLANES = 8
VMEM_LIMIT_BYTES = 56 * 1024 * 1024
NEG_BIG = -1e30


def _cparams(n_axes):
    return pltpu.CompilerParams(dimension_semantics=("arbitrary",) * n_axes,
                                vmem_limit_bytes=VMEM_LIMIT_BYTES)


def _sigmoid(x):
    return 1.0 / (1.0 + jnp.exp(-x))


def _silu(x):
    return x * _sigmoid(x)


def _rmsnorm_kernel(x_ref, g_ref, o_ref):
    x = x_ref[...]
    ms = jnp.mean(x * x, axis=-1, keepdims=True)
    o_ref[...] = (x * lax.rsqrt(ms + NORM_EPS) * g_ref[...]).astype(o_ref.dtype)


def rmsnorm(x, gain, tm):
    m, d = x.shape
    return pl.pallas_call(
        _rmsnorm_kernel,
        grid=(m // tm,),
        in_specs=[pl.BlockSpec((tm, d), lambda i: (i, 0)),
                  pl.BlockSpec((1, d), lambda i: (0, 0))],
        out_specs=pl.BlockSpec((tm, d), lambda i: (i, 0)),
        out_shape=jax.ShapeDtypeStruct((m, d), BF16),
        compiler_params=_cparams(1),
        name="rmsnorm",
    )(x, gain.reshape(1, d))


def _epi_plain(acc, ins, outs):
    outs[0][...] = acc.astype(outs[0].dtype)


def _epi_residual(acc, ins, outs):
    outs[0][...] = ins[0][...] + acc


def _epi_headnorm(acc, ins, outs):
    gain = ins[0][...]
    for c in range(acc.shape[1] // HEAD_DIM):
        blk = acc[:, c * HEAD_DIM:(c + 1) * HEAD_DIM]
        ms = jnp.mean(blk * blk, axis=-1, keepdims=True)
        y = blk * lax.rsqrt(ms + NORM_EPS) * gain[:, c * HEAD_DIM:(c + 1) * HEAD_DIM]
        outs[0][:, c * HEAD_DIM:(c + 1) * HEAD_DIM] = y.astype(outs[0].dtype)


def _epi_ple(acc, ins, outs):
    p_ref, wp_ref, x_ref = ins
    proj = jnp.dot(p_ref[...], wp_ref[...], preferred_element_type=F32)
    outs[0][...] = x_ref[...] + _sigmoid(acc) * proj


def _mm_kernel(*refs, n_in, epilogue):
    acc = jnp.dot(refs[0][...], refs[1][...], preferred_element_type=F32)
    epilogue(acc, refs[2:n_in], refs[n_in:])


def matmul(a, w, *, tm, tn, epilogue=_epi_plain, extras=(), extra_specs=(),
           out_dtype=F32, name="matmul"):
    m, k = a.shape
    n = w.shape[1]
    assert m % tm == 0 and n % tn == 0, (m, tm, n, tn)
    in_specs = [pl.BlockSpec((tm, k), lambda i, j: (i, 0)),
                pl.BlockSpec((k, tn), lambda i, j: (0, j))] + list(extra_specs)
    return pl.pallas_call(
        functools.partial(_mm_kernel, n_in=2 + len(extras), epilogue=epilogue),
        grid=(m // tm, n // tn),
        in_specs=in_specs,
        out_specs=pl.BlockSpec((tm, tn), lambda i, j: (i, j)),
        out_shape=jax.ShapeDtypeStruct((m, n), out_dtype),
        compiler_params=_cparams(2),
        name=name,
    )(a, w, *extras)


def mm_residual(a, w, res, *, tm, tn, name):
    return matmul(a, w, tm=tm, tn=tn, epilogue=_epi_residual, extras=(res,),
                  extra_specs=(pl.BlockSpec((tm, tn), lambda i, j: (i, j)),), name=name)


def mm_headnorm(a, w, gain_row, *, tm, tn, out_dtype, name):
    return matmul(a, w, tm=tm, tn=tn, epilogue=_epi_headnorm, extras=(gain_row,),
                  extra_specs=(pl.BlockSpec((1, tn), lambda i, j: (0, j)),),
                  out_dtype=out_dtype, name=name)


def mm_ple(h, wg, p, wp, x, *, tm, tn):
    pd = p.shape[1]
    return matmul(h, wg, tm=tm, tn=tn, epilogue=_epi_ple, extras=(p, wp, x),
                  extra_specs=(pl.BlockSpec((tm, pd), lambda i, j: (i, 0)),
                               pl.BlockSpec((pd, tn), lambda i, j: (0, j)),
                               pl.BlockSpec((tm, tn), lambda i, j: (i, j))),
                  name="ple_gate")


def _ffn_gate_kernel(u1_ref, u2_ref, h1_ref, h2_ref, s1_ref, s2_ref, w1_ref, w2_ref,
                     b1_ref, b2_ref, o_ref, cat1_ref, cat2_ref):
    tb = pl.program_id(1)
    tt = u1_ref.shape[0]

    def conv(u_ref, halo_ref, state_ref, w_ref, b_ref, cat_ref):
        @pl.when(tb == 0)
        def _():
            cat_ref[0:SUBLANES, :] = state_ref[...]

        @pl.when(tb > 0)
        def _():
            cat_ref[0:SUBLANES, :] = halo_ref[...]

        cat_ref[SUBLANES:SUBLANES + tt, :] = u_ref[...]
        w = w_ref[...]
        y = cat_ref[pl.ds(SUBLANES - 2, tt), :] * w[0:1, :] + b_ref[...]
        y = y + cat_ref[pl.ds(SUBLANES - 1, tt), :] * w[1:2, :]
        y = y + cat_ref[pl.ds(SUBLANES, tt), :] * w[2:3, :]
        return y

    y1 = conv(u1_ref, h1_ref, s1_ref, w1_ref, b1_ref, cat1_ref)
    y2 = conv(u2_ref, h2_ref, s2_ref, w2_ref, b2_ref, cat2_ref)
    o_ref[...] = (_silu(y1) * y2).astype(o_ref.dtype)


def ffn_gate(u, state8, conv_w, conv_b, *, tt, tc):
    b, t, c2 = u.shape
    dff = c2 // 2
    nj = dff // tc
    hb = tt // SUBLANES
    cur = lambda off: pl.BlockSpec((None, tt, tc), lambda bi, ti, j: (bi, ti, j + off))
    halo = lambda off: pl.BlockSpec(
        (None, SUBLANES, tc), lambda bi, ti, j: (bi, jnp.maximum(ti * hb - 1, 0), j + off))
    st = lambda off: pl.BlockSpec((None, SUBLANES, tc), lambda bi, ti, j: (bi, 0, j + off))
    wsp = lambda off: pl.BlockSpec((FFN_CONV, tc), lambda bi, ti, j: (0, j + off))
    bsp = lambda off: pl.BlockSpec((1, tc), lambda bi, ti, j: (0, j + off))
    return pl.pallas_call(
        _ffn_gate_kernel,
        grid=(b, t // tt, nj),
        in_specs=[cur(0), cur(nj), halo(0), halo(nj), st(0), st(nj), wsp(0), wsp(nj),
                  bsp(0), bsp(nj)],
        out_specs=pl.BlockSpec((None, tt, tc), lambda bi, ti, j: (bi, ti, j)),
        out_shape=jax.ShapeDtypeStruct((b, t, dff), BF16),
        scratch_shapes=[pltpu.VMEM((SUBLANES + tt, tc), F32),
                        pltpu.VMEM((SUBLANES + tt, tc), F32)],
        compiler_params=_cparams(3),
        name="ffn_gate",
    )(u, u, u, u, state8, state8, conv_w, conv_w, conv_b.reshape(1, c2), conv_b.reshape(1, c2))


def _ssd_kernel(z_ref, x_ref, xh_ref, b_ref, bh_ref, c_ref, ch_ref, dt_ref,
                sx_ref, sb_ref, sc_ref, wx_ref, wb_ref, wc_ref, bx_ref, bb_ref, bc_ref,
                dtb_ref, alog_ref, dsk_ref, ng_ref, h0_ref,
                y_ref, hl_ref,
                st_ref, catx_ref, catb_ref, catc_ref, yacc_ref, *, t_valid, n_chunks):
    g = pl.program_id(1)
    c = pl.program_id(2)
    L = SSD_CHUNK
    halo0 = SUBLANES - (SSM_CONV - 1)

    def conv_silu(cur_ref, halo_ref, state_ref, cat_ref, w_ref, bias_ref):
        @pl.when(c == 0)
        def _():
            cat_ref[0:SUBLANES, :] = state_ref[...]

        @pl.when(c > 0)
        def _():
            cat_ref[0:SUBLANES, :] = halo_ref[...]

        cat_ref[SUBLANES:SUBLANES + L, :] = cur_ref[...]
        w = w_ref[...]
        y = cat_ref[pl.ds(halo0, L), :] * w[0:1, :] + bias_ref[...]
        for k in range(1, SSM_CONV):
            y = y + cat_ref[pl.ds(halo0 + k, L), :] * w[k:k + 1, :]
        return _silu(y)

    xs = conv_silu(x_ref, xh_ref, sx_ref, catx_ref, wx_ref, bx_ref)
    bm = conv_silu(b_ref, bh_ref, sb_ref, catb_ref, wb_ref, bb_ref)
    cm = conv_silu(c_ref, ch_ref, sc_ref, catc_ref, wc_ref, bc_ref)

    @pl.when(c == 0)
    def _():
        st_ref[...] = h0_ref[...].T

    dtr = dt_ref[...] + dtb_ref[...]
    dt = jnp.maximum(dtr, 0.0) + jnp.log(1.0 + jnp.exp(-jnp.abs(dtr)))
    row = lax.broadcasted_iota(jnp.int32, (L, LANES), 0)
    dt = jnp.where(row + c * L < t_valid, dt, 0.0)
    ac = dt * (-jnp.exp(alog_ref[...]))
    s = 1
    while s < L:
        ac = ac + jnp.where(row >= s, pltpu.roll(ac, s, axis=0), 0.0)
        s *= 2
    shift = (LANES - HEADS_PER_GROUP * g) % LANES
    ac_g = pltpu.roll(ac, shift, axis=1)
    dt_g = pltpu.roll(dt, shift, axis=1)
    ac_t = ac_g.T
    dt_t = dt_g.T

    bmb = bm.astype(BF16)
    cmb = cm.astype(BF16)
    cb = lax.dot_general(cmb, bmb, (((1,), (1,)), ((), ())), preferred_element_type=F32)
    bt = bm.T
    st_prev = st_ref[...]
    yoff = jnp.dot(cmb, st_prev.astype(BF16), preferred_element_type=F32)

    col = lax.broadcasted_iota(jnp.int32, (L, L), 1)
    causal = row >= col
    lane = lax.broadcasted_iota(jnp.int32, (1, LANES), 1)
    low = lane < SSM_HEAD_DIM
    ssq = jnp.zeros((L, LANES), F32)
    for j in range(HEADS_PER_GROUP // 2):
        sl = slice(j * LANES, (j + 1) * LANES)
        xp = xs[:, sl]
        xpb = xp.astype(BF16)
        parts = []
        for e in range(2):
            h = 2 * j + e
            ac_row = ac_t[h:h + 1, :]
            dt_row = dt_t[h:h + 1, :]
            ac_col = jnp.broadcast_to(ac_g[:, h:h + 1], (L, LANES))
            ac_last = ac_col[L - 1:L, :]
            decay = jnp.exp(jnp.where(causal, ac_col - ac_row, NEG_BIG))
            mh = (cb * decay * dt_row).astype(BF16)
            ydiag = jnp.dot(mh, xpb, preferred_element_type=F32)
            w_row = dt_row * jnp.exp(ac_last - ac_row)
            btw = (bt * w_row).astype(BF16)
            stc = jnp.dot(btw, xpb, preferred_element_type=F32)
            parts.append((ydiag, stc, jnp.exp(ac_col), jnp.exp(ac_last)))
        ydiag = jnp.where(low, parts[0][0], parts[1][0])
        stc = jnp.where(low, parts[0][1], parts[1][1])
        ecol = jnp.where(low, parts[0][2], parts[1][2])
        dec = jnp.where(low, parts[0][3], parts[1][3])
        y = ydiag + ecol * yoff[:, sl] + xp * dsk_ref[:, sl]
        y = y * _silu(z_ref[:, sl])
        ssq = ssq + y * y
        yacc_ref[:, sl] = y
        st_ref[:, sl] = dec * st_prev[:, sl] + stc

    ms = jnp.sum(ssq, axis=-1, keepdims=True) * (1.0 / GROUP_WIDTH)
    rinv = lax.rsqrt(ms + NORM_EPS)
    y_ref[...] = (yacc_ref[...] * rinv * ng_ref[...]).astype(y_ref.dtype)

    @pl.when(c == n_chunks - 1)
    def _():
        hl_ref[...] = st_ref[...].T


def ssd_mixer(zx, dtraw, conv8, h0, conv_w, conv_b, dt_bias, a_log, d_skip, norm_g, *, t_valid):
    b, tp, _ = zx.shape
    nc = tp // SSD_CHUNK
    L, gw, n = SSD_CHUNK, GROUP_WIDTH, D_STATE
    hb = L // SUBLANES
    xoff = D_INNER // gw
    boff = (2 * D_INNER) // n
    coff = boff + N_SSM_GROUPS
    sboff = D_INNER // n
    scoff = sboff + N_SSM_GROUPS
    prev = lambda ci: jnp.maximum(ci * hb - 1, 0)
    in_specs = [
        pl.BlockSpec((None, L, gw), lambda bi, gi, ci: (bi, ci, gi)),
        pl.BlockSpec((None, L, gw), lambda bi, gi, ci: (bi, ci, xoff + gi)),
        pl.BlockSpec((None, SUBLANES, gw), lambda bi, gi, ci: (bi, prev(ci), xoff + gi)),
        pl.BlockSpec((None, L, n), lambda bi, gi, ci: (bi, ci, boff + gi)),
        pl.BlockSpec((None, SUBLANES, n), lambda bi, gi, ci: (bi, prev(ci), boff + gi)),
        pl.BlockSpec((None, L, n), lambda bi, gi, ci: (bi, ci, coff + gi)),
        pl.BlockSpec((None, SUBLANES, n), lambda bi, gi, ci: (bi, prev(ci), coff + gi)),
        pl.BlockSpec((None, L, N_SSM_HEADS), lambda bi, gi, ci: (bi, ci, 0)),
        pl.BlockSpec((None, SUBLANES, gw), lambda bi, gi, ci: (bi, 0, gi)),
        pl.BlockSpec((None, SUBLANES, n), lambda bi, gi, ci: (bi, 0, sboff + gi)),
        pl.BlockSpec((None, SUBLANES, n), lambda bi, gi, ci: (bi, 0, scoff + gi)),
        pl.BlockSpec((SSM_CONV, gw), lambda bi, gi, ci: (0, gi)),
        pl.BlockSpec((SSM_CONV, n), lambda bi, gi, ci: (0, sboff + gi)),
        pl.BlockSpec((SSM_CONV, n), lambda bi, gi, ci: (0, scoff + gi)),
        pl.BlockSpec((1, gw), lambda bi, gi, ci: (0, gi)),
        pl.BlockSpec((1, n), lambda bi, gi, ci: (0, sboff + gi)),
        pl.BlockSpec((1, n), lambda bi, gi, ci: (0, scoff + gi)),
        pl.BlockSpec((1, N_SSM_HEADS), lambda bi, gi, ci: (0, 0)),
        pl.BlockSpec((1, N_SSM_HEADS), lambda bi, gi, ci: (0, 0)),
        pl.BlockSpec((1, gw), lambda bi, gi, ci: (0, gi)),
        pl.BlockSpec((1, gw), lambda bi, gi, ci: (0, gi)),
        pl.BlockSpec((None, None, gw, n), lambda bi, gi, ci: (bi, gi, 0, 0)),
    ]
    d_rep = jnp.repeat(d_skip, SSM_HEAD_DIM).reshape(1, D_INNER)
    cb2 = conv_b.reshape(1, CONV_DIM)
    y, hl = pl.pallas_call(
        functools.partial(_ssd_kernel, t_valid=t_valid, n_chunks=nc),
        grid=(b, N_SSM_GROUPS, nc),
        in_specs=in_specs,
        out_specs=[pl.BlockSpec((None, L, gw), lambda bi, gi, ci: (bi, ci, gi)),
                   pl.BlockSpec((None, None, gw, n), lambda bi, gi, ci: (bi, gi, 0, 0))],
        out_shape=[jax.ShapeDtypeStruct((b, tp, D_INNER), BF16),
                   jax.ShapeDtypeStruct((b, N_SSM_GROUPS, gw, n), F32)],
        scratch_shapes=[pltpu.VMEM((n, gw), F32),
                        pltpu.VMEM((SUBLANES + L, gw), F32),
                        pltpu.VMEM((SUBLANES + L, n), F32),
                        pltpu.VMEM((SUBLANES + L, n), F32),
                        pltpu.VMEM((L, gw), F32)],
        compiler_params=_cparams(3),
        name="ssd_mixer",
    )(zx, zx, zx, zx, zx, zx, zx, dtraw, conv8, conv8, conv8, conv_w, conv_w, conv_w,
      cb2, cb2, cb2, dt_bias.reshape(1, -1), a_log.reshape(1, -1), d_rep,
      norm_g.reshape(1, D_INNER), h0.reshape(b, N_SSM_GROUPS, gw, n))
    return y, hl.reshape(b, N_SSM_HEADS, SSM_HEAD_DIM, n)


def _rel_bucket(dist):
    max_exact = N_BUCKETS // 2
    nn = np.asarray(dist, dtype=np.int64)
    ratio = np.log(np.maximum(nn, 1) / max_exact) / np.log(REL_MAX_DIST / max_exact)
    large = np.minimum(max_exact + (np.maximum(ratio, 0.0) * (N_BUCKETS - max_exact)).astype(np.int64),
                       N_BUCKETS - 1)
    return np.where(nn < max_exact, nn, large).astype(np.int32)


def _bias_by_step(rel_bias, g):
    dists = np.arange(Q_BLOCK + 1) * DIL_RATES[g]
    return rel_bias[_rel_bucket(dists)][:, g * N_SLOTS:(g + 1) * N_SLOTS].astype(F32)


def _toeplitz_bias(rel_bias):
    ql = np.arange(Q_BLOCK)[:, None]
    kl = np.arange(2 * Q_BLOCK)[None, :]
    step = ql + Q_BLOCK - kl
    ok = (step >= 0) & (step <= Q_BLOCK)
    idx = np.clip(step, 0, Q_BLOCK)
    tiles = []
    for g in range(N_DIL_GROUPS):
        tab = _bias_by_step(rel_bias, g)
        t = jnp.where(ok[None], jnp.transpose(tab[idx], (2, 0, 1)), NEG_BIG)
        tiles.append(t)
    return jnp.stack(tiles)


def _attn_group_kernel(q_ref, kc_ref, kp_ref, vc_ref, vp_ref, bias_ref, o_ref, lse_ref):
    qb = pl.program_id(3)
    q = q_ref[...]
    qs = jnp.concatenate([q[:, r * HEAD_DIM:(r + 1) * HEAD_DIM] for r in range(KV_REP)], axis=0)
    k2 = jnp.concatenate([kp_ref[...], kc_ref[...]], axis=0).astype(BF16)
    v2 = jnp.concatenate([vp_ref[...], vc_ref[...]], axis=0).astype(BF16)
    s = lax.dot_general(qs, k2, (((1,), (1,)), ((), ())), preferred_element_type=F32)
    s = s + bias_ref[...].reshape(KV_REP * Q_BLOCK, 2 * Q_BLOCK)
    col = lax.broadcasted_iota(jnp.int32, s.shape, 1)
    s = jnp.where((col >= Q_BLOCK) | (qb > 0), s, NEG_BIG)
    m = jnp.max(s, axis=-1, keepdims=True)
    p = jnp.exp(s - m)
    l = jnp.sum(p, axis=-1, keepdims=True)
    o = jnp.dot(p.astype(BF16), v2, preferred_element_type=F32) / l
    o_ref[...] = jnp.concatenate([o[r * Q_BLOCK:(r + 1) * Q_BLOCK] for r in range(KV_REP)], axis=1)
    lse = m + jnp.log(l)
    lane = lax.broadcasted_iota(jnp.int32, (Q_BLOCK, LANES), 1)
    seg = LANES // KV_REP
    tile = jnp.broadcast_to(lse[(KV_REP - 1) * Q_BLOCK:], (Q_BLOCK, LANES))
    for r in range(KV_REP - 2, -1, -1):
        tile = jnp.where(lane < (r + 1) * seg, lse[r * Q_BLOCK:(r + 1) * Q_BLOCK], tile)
    lse_ref[...] = tile


def _attn_group(q, k, v, bias_t, g, b, t):
    r = DIL_RATES[g]
    tsub = t // r
    nqb = tsub // Q_BLOCK
    rw = KV_REP * HEAD_DIM
    qv = q.reshape(b, tsub, r * N_DIL_GROUPS * N_SLOTS * HEAD_DIM)
    kv = k.reshape(b, tsub, r * N_KV_HEADS * HEAD_DIM)
    vv = v.reshape(b, tsub, r * N_KV_HEADS * HEAD_DIM)
    qcols = N_DIL_GROUPS * N_KV_HEADS
    cur = pl.BlockSpec((None, Q_BLOCK, HEAD_DIM),
                       lambda bi, hi, ri, qi: (bi, qi, ri * N_KV_HEADS + hi))
    prv = pl.BlockSpec((None, Q_BLOCK, HEAD_DIM),
                       lambda bi, hi, ri, qi: (bi, jnp.maximum(qi - 1, 0), ri * N_KV_HEADS + hi))
    o, lse = pl.pallas_call(
        _attn_group_kernel,
        grid=(b, N_KV_HEADS, r, nqb),
        in_specs=[pl.BlockSpec((None, Q_BLOCK, rw),
                               lambda bi, hi, ri, qi: (bi, qi, ri * qcols + g * N_KV_HEADS + hi)),
                  cur, prv, cur, prv,
                  pl.BlockSpec((None, KV_REP, Q_BLOCK, 2 * Q_BLOCK),
                               lambda bi, hi, ri, qi: (g, hi, 0, 0))],
        out_specs=[pl.BlockSpec((None, Q_BLOCK, rw),
                                lambda bi, hi, ri, qi: (bi, qi, ri * N_KV_HEADS + hi)),
                   pl.BlockSpec((None, None, Q_BLOCK, LANES),
                                lambda bi, hi, ri, qi: (bi, hi, qi, ri))],
        out_shape=[jax.ShapeDtypeStruct((b, tsub, r * N_SLOTS * HEAD_DIM), F32),
                   jax.ShapeDtypeStruct((b, N_KV_HEADS, tsub, r * LANES), F32)],
        compiler_params=_cparams(4),
        name=f"attn_group{g}",
    )(qv, kv, kv, vv, vv, bias_t)
    return o.reshape(b * t, N_SLOTS * HEAD_DIM), lse.reshape(b, N_KV_HEADS, t, LANES)


def _attn_combine_kernel(o0_ref, o1_ref, o2_ref, l0_ref, l1_ref, l2_ref, out_ref):
    seg = LANES // KV_REP
    lses = (l0_ref[...], l1_ref[...], l2_ref[...])
    for r in range(KV_REP):
        cs = [x[:, r * seg:r * seg + 1] for x in lses]
        m = jnp.maximum(jnp.maximum(cs[0], cs[1]), cs[2])
        es = [jnp.exp(c - m) for c in cs]
        inv = 1.0 / (es[0] + es[1] + es[2])
        sl = slice(r * HEAD_DIM, (r + 1) * HEAD_DIM)
        o = (es[0] * inv) * o0_ref[:, sl] + (es[1] * inv) * o1_ref[:, sl] + (es[2] * inv) * o2_ref[:, sl]
        out_ref[:, sl] = o.astype(out_ref.dtype)


def _attn_combine(outs, lses, b, t):
    rw = KV_REP * HEAD_DIM
    nt = t // Q_BLOCK
    osp = pl.BlockSpec((Q_BLOCK, rw), lambda bi, hi, ti: (bi * nt + ti, hi))
    lsp = pl.BlockSpec((None, None, Q_BLOCK, LANES), lambda bi, hi, ti: (bi, hi, ti, 0))
    return pl.pallas_call(
        _attn_combine_kernel,
        grid=(b, N_KV_HEADS, nt),
        in_specs=[osp, osp, osp, lsp, lsp, lsp],
        out_specs=osp,
        out_shape=jax.ShapeDtypeStruct((b * t, N_SLOTS * HEAD_DIM), BF16),
        compiler_params=_cparams(3),
        name="attn_combine",
    )(*outs, *lses)


def attention_prompt(q, k, v, bias_t, b, t):
    outs, lses = [], []
    for g in range(N_DIL_GROUPS):
        o, lse = _attn_group(q, k, v, bias_t, g, b, t)
        outs.append(o)
        lses.append(lse)
    return _attn_combine(outs, lses, b, t)


def _attn_sample_kernel(q_ref, kc_ref, kn_ref, vc_ref, vn_ref, bm_ref, bs_ref, o_ref,
                        kbuf_ref, vbuf_ref, *, t_new):
    w = WINDOW_MAX
    rows = N_DIL_GROUPS * t_new * KV_REP
    kbuf_ref[0:w, :] = kc_ref[...]
    kbuf_ref[w:w + SUBLANES, :] = kn_ref[...]
    vbuf_ref[0:w, :] = vc_ref[...]
    vbuf_ref[w:w + SUBLANES, :] = vn_ref[...]
    q = q_ref[...]
    qf = q.astype(F32)
    ridx = lax.broadcasted_iota(jnp.int32, (rows, 1), 0)

    def row_mask(g, j):
        base = (g * t_new + j) * KV_REP
        return (ridx >= base) & (ridx < base + KV_REP)

    kself = jnp.zeros((rows, HEAD_DIM), F32)
    vself = jnp.zeros((rows, HEAD_DIM), F32)
    for j in range(t_new):
        tok_j = row_mask(0, j)
        for g in range(1, N_DIL_GROUPS):
            tok_j = tok_j | row_mask(g, j)
        kself = jnp.where(tok_j, kn_ref[j:j + 1, :].astype(BF16).astype(F32), kself)
        vself = jnp.where(tok_j, vn_ref[j:j + 1, :], vself)
    s_self = jnp.sum(qf * kself, axis=-1, keepdims=True) + bs_ref[:, 0:1]
    s = jnp.zeros((rows, Q_BLOCK), F32)
    for g in range(N_DIL_GROUPS):
        for j in range(t_new):
            start = w + j - Q_BLOCK * DIL_RATES[g]
            ks = kbuf_ref[pl.ds(start, Q_BLOCK, stride=DIL_RATES[g]), :].astype(BF16)
            sg = lax.dot_general(q, ks, (((1,), (1,)), ((), ())), preferred_element_type=F32)
            s = jnp.where(row_mask(g, j), sg, s)
    s = s + bm_ref[...]
    m = jnp.maximum(jnp.max(s, axis=-1, keepdims=True), s_self)
    p = jnp.exp(s - m)
    p_self = jnp.exp(s_self - m)
    l = jnp.sum(p, axis=-1, keepdims=True) + p_self
    pb = p.astype(BF16)
    acc = p_self * vself
    for g in range(N_DIL_GROUPS):
        for j in range(t_new):
            start = w + j - Q_BLOCK * DIL_RATES[g]
            vs = vbuf_ref[pl.ds(start, Q_BLOCK, stride=DIL_RATES[g]), :].astype(BF16)
            og = jnp.dot(pb, vs, preferred_element_type=F32)
            acc = acc + jnp.where(row_mask(g, j), og, 0.0)
    o = acc / l
    lse = m + jnp.log(l)
    n = t_new * KV_REP
    ls = [lse[g * n:(g + 1) * n] for g in range(N_DIL_GROUPS)]
    mm = jnp.maximum(jnp.maximum(ls[0], ls[1]), ls[2])
    es = [jnp.exp(x - mm) for x in ls]
    inv = 1.0 / (es[0] + es[1] + es[2])
    out = sum((es[g] * inv) * o[g * n:(g + 1) * n] for g in range(N_DIL_GROUPS))
    o_ref[...] = out.astype(o_ref.dtype)


def attention_sample(q, k_new, v_new, k_cache, v_cache, rel_bias, b, t_new):
    assert PAST_LEN >= WINDOW_MAX and t_new <= SUBLANES
    rows = N_DIL_GROUPS * t_new * KV_REP
    n = t_new * KV_REP
    qs = q.reshape(b, t_new, N_DIL_GROUPS, N_KV_HEADS, KV_REP, HEAD_DIM)
    qs = jnp.transpose(qs, (0, 3, 2, 1, 4, 5)).reshape(b, N_KV_HEADS, rows, HEAD_DIM)
    pad = ((0, 0), (0, SUBLANES - t_new), (0, 0))
    kn = jnp.pad(k_new, pad)
    vn = jnp.pad(v_new, pad)
    bm, bs = [], []
    for g in range(N_DIL_GROUPS):
        tab = _bias_by_step(rel_bias, g)
        tab = tab.reshape(Q_BLOCK + 1, N_KV_HEADS, KV_REP)
        main = jnp.transpose(tab[1:][::-1], (1, 2, 0))
        bm.append(jnp.broadcast_to(main[:, None], (N_KV_HEADS, t_new, KV_REP, Q_BLOCK)))
        bs.append(jnp.broadcast_to(tab[0][:, None, :, None], (N_KV_HEADS, t_new, KV_REP, LANES)))
    bm = jnp.stack(bm, axis=1).reshape(N_KV_HEADS, rows, Q_BLOCK)
    bs = jnp.stack(bs, axis=1).reshape(N_KV_HEADS, rows, LANES)
    cache = pl.BlockSpec((None, WINDOW_MAX, HEAD_DIM), lambda bi, hi: (bi, 0, hi))
    new = pl.BlockSpec((None, SUBLANES, HEAD_DIM), lambda bi, hi: (bi, 0, hi))
    o = pl.pallas_call(
        functools.partial(_attn_sample_kernel, t_new=t_new),
        grid=(b, N_KV_HEADS),
        in_specs=[pl.BlockSpec((None, None, rows, HEAD_DIM), lambda bi, hi: (bi, hi, 0, 0)),
                  cache, new, cache, new,
                  pl.BlockSpec((None, rows, Q_BLOCK), lambda bi, hi: (hi, 0, 0)),
                  pl.BlockSpec((None, rows, LANES), lambda bi, hi: (hi, 0, 0))],
        out_specs=pl.BlockSpec((None, None, n, HEAD_DIM), lambda bi, hi: (bi, hi, 0, 0)),
        out_shape=jax.ShapeDtypeStruct((b, N_KV_HEADS, n, HEAD_DIM), BF16),
        scratch_shapes=[pltpu.VMEM((WINDOW_MAX + SUBLANES, HEAD_DIM), F32),
                        pltpu.VMEM((WINDOW_MAX + SUBLANES, HEAD_DIM), F32)],
        compiler_params=_cparams(2),
        name="attn_sample",
    )(qs, k_cache, kn, v_cache, vn, bm, bs)
    o = o.reshape(b, N_KV_HEADS, t_new, KV_REP, HEAD_DIM)
    return jnp.transpose(o, (0, 2, 1, 3, 4)).reshape(b * t_new, N_SLOTS * HEAD_DIM)


def _pad_rows_front(state, rows):
    return jnp.pad(state, ((0, 0), (rows - state.shape[1], 0), (0, 0)))


def _trunk(x, p, b, t, ssm0, convs0, convf0, k_cache, v_cache, W, *, prompt):
    m = b * t
    big = m >= 1024
    tm = 1024 if big else m
    tm_k = 512 if big else m
    tnorm = 256 if big else m
    ssm_out, convs_out, convf_out = [], [], []
    k = v = None
    for i in range(DEPTH):
        if i < N_A_LAYERS:
            h = rmsnorm(x, W["norm_mix"][i], tnorm)
            zx = matmul(h, W["ssm_w_in_main"][i], tm=tm, tn=1024, name="ssm_in")
            dtraw = matmul(h, W["ssm_w_in_dt"][i], tm=tm, tn=N_SSM_HEADS, name="ssm_in_dt")
            zx = zx.reshape(b, t, XBC_END)
            dtraw = dtraw.reshape(b, t, N_SSM_HEADS)
            convs_out.append(zx[:, t - (SSM_CONV - 1):, D_INNER:])
            tp = -(-t // SSD_CHUNK) * SSD_CHUNK
            if tp != t:
                zx = jnp.pad(zx, ((0, 0), (0, tp - t), (0, 0)))
                dtraw = jnp.pad(dtraw, ((0, 0), (0, tp - t), (0, 0)))
            y, hs = ssd_mixer(zx, dtraw, _pad_rows_front(convs0[i], SUBLANES), ssm0[i],
                              W["ssm_conv_w"][i], W["ssm_conv_b"][i], W["ssm_dt_bias"][i],
                              W["ssm_a_log"][i], W["ssm_d"][i], W["ssm_norm"][i], t_valid=t)
            ssm_out.append(hs)
            y = y[:, :t].reshape(m, D_INNER)
            x = mm_residual(y, W["ssm_w_out"][i], x, tm=tm_k, tn=512, name="ssm_out")
        else:
            li = i - N_A_LAYERS
            if li == 0:
                hkv = rmsnorm(x, W["norm_kv"], tnorm)
                k = mm_headnorm(hkv, W["w_k"], W["k_gain"], tm=tm, tn=1024, out_dtype=F32, name="k_proj")
                v = matmul(hkv, W["w_v"], tm=tm, tn=1024, name="v_proj")
                k = k.reshape(b, t, N_KV_HEADS * HEAD_DIM)
                v = v.reshape(b, t, N_KV_HEADS * HEAD_DIM)
            h = rmsnorm(x, W["norm_mix"][i], tnorm)
            q = mm_headnorm(h, W["attn_w_q"][li], W["q_gain"][li], tm=tm, tn=1024, out_dtype=BF16,
                            name="q_proj")
            if prompt:
                o = attention_prompt(q, k, v, W["bias_tiles"], b, t)
            else:
                o = attention_sample(q, k, v, k_cache, v_cache, W["rel_bias"], b, t)
            x = mm_residual(o, W["attn_w_o"][li], x, tm=tm, tn=512, name="attn_out")
        h = rmsnorm(x, W["norm_ffn"][i], tnorm)
        u = matmul(h, W["ffn_w_up"][i], tm=tm, tn=512, name="ffn_up").reshape(b, t, 2 * D_FF)
        convf_out.append(u[:, t - (FFN_CONV - 1):])
        tpad = -(-t // SUBLANES) * SUBLANES
        if tpad != t:
            u = jnp.pad(u, ((0, 0), (0, tpad - t), (0, 0)))
        gte = ffn_gate(u, _pad_rows_front(convf0[i], SUBLANES), W["ffn_conv_w"][i], W["ffn_conv_b"][i],
                       tt=min(512, tpad), tc=256)
        gte = gte[:, :t].reshape(m, D_FF)
        x = mm_residual(gte, W["ffn_w_down"][i], x, tm=tm_k, tn=256, name="ffn_down")
        h = rmsnorm(x, W["norm_ple"][i], tnorm)
        x = mm_ple(h, W["ple_w_gate"][i], p[i], W["ple_w_proj"][i], x, tm=tm, tn=512)
    return x, jnp.stack(ssm_out), jnp.stack(convs_out), jnp.stack(convf_out), k, v


def kernel(x_prompt, x_sample, p_prompt, p_sample, state_ssm, state_conv_ssm, state_conv_ffn, cache_k_window, cache_v_window, norm_mix, norm_ffn, norm_ple, ssm_w_in, ssm_conv_w, ssm_conv_b, ssm_dt_bias, ssm_a_log, ssm_d, ssm_norm, ssm_w_out, norm_kv, w_k, w_v, k_norm, attn_w_q, q_norm, attn_w_o, rel_bias, ffn_w_up, ffn_conv_w, ffn_conv_b, ffn_w_down, ple_w_proj, ple_w_gate):
    bp, tp, _ = x_prompt.shape
    bs, ts, _ = x_sample.shape
    assert tp % (Q_BLOCK * max(DIL_RATES)) == 0 and tp <= WINDOW_MAX
    assert cache_k_window.shape[1] == WINDOW_MAX
    q_gain = jnp.broadcast_to(q_norm[:, :, None, :], (DEPTH - N_A_LAYERS, N_DIL_GROUPS, N_SLOTS, HEAD_DIM))
    W = dict(
        norm_mix=norm_mix, norm_ffn=norm_ffn, norm_ple=norm_ple, norm_kv=norm_kv,
        ssm_w_in_main=ssm_w_in[:, :, :XBC_END].astype(BF16),
        ssm_w_in_dt=ssm_w_in[:, :, XBC_END:].astype(BF16),
        ssm_conv_w=ssm_conv_w, ssm_conv_b=ssm_conv_b, ssm_dt_bias=ssm_dt_bias, ssm_a_log=ssm_a_log,
        ssm_d=ssm_d, ssm_norm=ssm_norm, ssm_w_out=ssm_w_out.astype(BF16),
        w_k=w_k.astype(BF16), w_v=w_v.astype(BF16),
        k_gain=jnp.tile(k_norm, N_KV_HEADS).reshape(1, -1),
        attn_w_q=attn_w_q.astype(BF16),
        q_gain=(q_gain * (HEAD_DIM ** -0.5)).reshape(DEPTH - N_A_LAYERS, 1, -1),
        attn_w_o=attn_w_o.astype(BF16), rel_bias=rel_bias, bias_tiles=_toeplitz_bias(rel_bias),
        ffn_w_up=ffn_w_up.astype(BF16), ffn_conv_w=ffn_conv_w, ffn_conv_b=ffn_conv_b,
        ffn_w_down=ffn_w_down.astype(BF16), ple_w_proj=ple_w_proj.astype(BF16),
        ple_w_gate=ple_w_gate.astype(BF16))

    f = lambda a: jnp.zeros(a.shape[:1] + (bp,) + a.shape[2:], a.dtype)
    y_p, ssm_p, convs_p, convf_p, k_p, v_p = _trunk(
        x_prompt.reshape(bp * tp, D_MODEL), p_prompt.reshape(DEPTH, bp * tp, PLE_DIM).astype(BF16),
        bp, tp, f(state_ssm), f(state_conv_ssm), f(state_conv_ffn), None, None, W, prompt=True)
    kc = cache_k_window.reshape(bs, WINDOW_MAX, N_KV_HEADS * HEAD_DIM)
    vc = cache_v_window.reshape(bs, WINDOW_MAX, N_KV_HEADS * HEAD_DIM)
    y_s, ssm_s, convs_s, convf_s, k_s, v_s = _trunk(
        x_sample.reshape(bs * ts, D_MODEL), p_sample.reshape(DEPTH, bs * ts, PLE_DIM).astype(BF16),
        bs, ts, state_ssm, state_conv_ssm, state_conv_ffn, kc, vc, W, prompt=False)
    kv4 = lambda a, b_, t_: a.reshape(b_, t_, N_KV_HEADS, HEAD_DIM)
    return (y_p.reshape(bp, tp, D_MODEL), y_s.reshape(bs, ts, D_MODEL), ssm_p, ssm_s, convs_p, convs_s,
            convf_p, convf_s, kv4(k_p, bp, tp), kv4(v_p, bp, tp), kv4(k_s, bs, ts), kv4(v_s, bs, ts))
```

```python
import functools

import numpy as np
import jax
import jax.numpy as jnp
from jax import lax
from jax.experimental import pallas as pl
from jax.experimental.pallas import tpu as pltpu

F32 = jnp.float32
BF16 = jnp.bfloat16

D_MODEL = 4096
DEPTH = 4
N_A_LAYERS = DEPTH // 2
PAST_LEN = 8192
D_INNER = 2 * D_MODEL
SSM_HEAD_DIM = 64
N_SSM_HEADS = D_INNER // SSM_HEAD_DIM
D_STATE = 128
N_SSM_GROUPS = 8
HEADS_PER_GROUP = N_SSM_HEADS // N_SSM_GROUPS
GROUP_WIDTH = D_INNER // N_SSM_GROUPS
SSM_CONV = 4
SSD_CHUNK = 128
CONV_DIM = D_INNER + 2 * N_SSM_GROUPS * D_STATE
XBC_END = D_INNER + CONV_DIM
HEAD_DIM = 128
N_SLOTS = D_MODEL // HEAD_DIM
N_KV_HEADS = 8
KV_REP = N_SLOTS // N_KV_HEADS
DIL_WINDOWS = (128, 512, 2048)
DIL_RATES = (1, 4, 16)
N_DIL_GROUPS = len(DIL_WINDOWS)
WINDOW_MAX = max(DIL_WINDOWS)
Q_BLOCK = 128
N_BUCKETS = 32
REL_MAX_DIST = WINDOW_MAX
D_FF = 256 * ((8 * D_MODEL // 3 + 255) // 256)
FFN_CONV = 3
PLE_DIM = 256
NORM_EPS = 1e-6

LANES = 128
SUBLANES = 8
VMEM_LIMIT_BYTES = 56 * 1024 * 1024
NEG_BIG = -1e30


def _cparams(n_axes):
    return pltpu.CompilerParams(dimension_semantics=("arbitrary",) * n_axes,
                                vmem_limit_bytes=VMEM_LIMIT_BYTES)


def _sigmoid(x):
    return 1.0 / (1.0 + jnp.exp(-x))


def _silu(x):
    return x * _sigmoid(x)


def _rmsnorm_kernel(x_ref, g_ref, o_ref):
    x = x_ref[...]
    ms = jnp.mean(x * x, axis=-1, keepdims=True)
    o_ref[...] = (x * lax.rsqrt(ms + NORM_EPS) * g_ref[...]).astype(o_ref.dtype)


def rmsnorm(x, gain, tm):
    m, d = x.shape
    return pl.pallas_call(
        _rmsnorm_kernel,
        grid=(m // tm,),
        in_specs=[pl.BlockSpec((tm, d), lambda i: (i, 0)),
                  pl.BlockSpec((1, d), lambda i: (0, 0))],
        out_specs=pl.BlockSpec((tm, d), lambda i: (i, 0)),
        out_shape=jax.ShapeDtypeStruct((m, d), BF16),
        compiler_params=_cparams(1),
        name="rmsnorm",
    )(x, gain.reshape(1, d))


def _epi_plain(acc, ins, outs):
    outs[0][...] = acc.astype(outs[0].dtype)


def _epi_residual(acc, ins, outs):
    outs[0][...] = ins[0][...] + acc


def _epi_headnorm(acc, ins, outs):
    gain = ins[0][...]
    for c in range(acc.shape[1] // HEAD_DIM):
        blk = acc[:, c * HEAD_DIM:(c + 1) * HEAD_DIM]
        ms = jnp.mean(blk * blk, axis=-1, keepdims=True)
        y = blk * lax.rsqrt(ms + NORM_EPS) * gain[:, c * HEAD_DIM:(c + 1) * HEAD_DIM]
        outs[0][:, c * HEAD_DIM:(c + 1) * HEAD_DIM] = y.astype(outs[0].dtype)


def _epi_ple(acc, ins, outs):
    p_ref, wp_ref, x_ref = ins
    proj = jnp.dot(p_ref[...], wp_ref[...], preferred_element_type=F32)
    outs[0][...] = x_ref[...] + _sigmoid(acc) * proj


def _mm_kernel(*refs, n_in, epilogue):
    acc = jnp.dot(refs[0][...], refs[1][...], preferred_element_type=F32)
    epilogue(acc, refs[2:n_in], refs[n_in:])


def matmul(a, w, *, tm, tn, epilogue=_epi_plain, extras=(), extra_specs=(),
           out_dtype=F32, name="matmul"):
    m, k = a.shape
    n = w.shape[1]
    assert m % tm == 0 and n % tn == 0, (m, tm, n, tn)
    in_specs = [pl.BlockSpec((tm, k), lambda i, j: (i, 0)),
                pl.BlockSpec((k, tn), lambda i, j: (0, j))] + list(extra_specs)
    return pl.pallas_call(
        functools.partial(_mm_kernel, n_in=2 + len(extras), epilogue=epilogue),
        grid=(m // tm, n // tn),
        in_specs=in_specs,
        out_specs=pl.BlockSpec((tm, tn), lambda i, j: (i, j)),
        out_shape=jax.ShapeDtypeStruct((m, n), out_dtype),
        compiler_params=_cparams(2),
        name=name,
    )(a, w, *extras)


def mm_residual(a, w, res, *, tm, tn, name):
    return matmul(a, w, tm=tm, tn=tn, epilogue=_epi_residual, extras=(res,),
                  extra_specs=(pl.BlockSpec((tm, tn), lambda i, j: (i, j)),), name=name)


def mm_headnorm(a, w, gain_row, *, tm, tn, out_dtype, name):
    return matmul(a, w, tm=tm, tn=tn, epilogue=_epi_headnorm, extras=(gain_row,),
                  extra_specs=(pl.BlockSpec((1, tn), lambda i, j: (0, j)),),
                  out_dtype=out_dtype, name=name)


def mm_ple(h, wg, p, wp, x, *, tm, tn):
    pd = p.shape[1]
    return matmul(h, wg, tm=tm, tn=tn, epilogue=_epi_ple, extras=(p, wp, x),
                  extra_specs=(pl.BlockSpec((tm, pd), lambda i, j: (i, 0)),
                               pl.BlockSpec((pd, tn), lambda i, j: (0, j)),
                               pl.BlockSpec((tm, tn), lambda i, j: (i, j))),
                  name="ple_gate")


def _ffn_gate_kernel(u1_ref, u2_ref, h1_ref, h2_ref, s1_ref, s2_ref, w1_ref, w2_ref,
                     b1_ref, b2_ref, o_ref, cat1_ref, cat2_ref):
    tb = pl.program_id(1)
    tt = u1_ref.shape[0]

    def conv(u_ref, halo_ref, state_ref, w_ref, b_ref, cat_ref):
        @pl.when(tb == 0)
        def _():
            cat_ref[0:SUBLANES, :] = state_ref[...]

        @pl.when(tb > 0)
        def _():
            cat_ref[0:SUBLANES, :] = halo_ref[...]

        cat_ref[SUBLANES:SUBLANES + tt, :] = u_ref[...]
        w = w_ref[...]
        y = cat_ref[pl.ds(SUBLANES - 2, tt), :] * w[0:1, :] + b_ref[...]
        y = y + cat_ref[pl.ds(SUBLANES - 1, tt), :] * w[1:2, :]
        y = y + cat_ref[pl.ds(SUBLANES, tt), :] * w[2:3, :]
        return y

    y1 = conv(u1_ref, h1_ref, s1_ref, w1_ref, b1_ref, cat1_ref)
    y2 = conv(u2_ref, h2_ref, s2_ref, w2_ref, b2_ref, cat2_ref)
    o_ref[...] = (_silu(y1) * y2).astype(o_ref.dtype)


def ffn_gate(u, state8, conv_w, conv_b, *, tt, tc):
    b, t, c2 = u.shape
    dff = c2 // 2
    nj = dff // tc
    hb = tt // SUBLANES
    cur = lambda off: pl.BlockSpec((None, tt, tc), lambda bi, ti, j: (bi, ti, j + off))
    halo = lambda off: pl.BlockSpec(
        (None, SUBLANES, tc), lambda bi, ti, j: (bi, jnp.maximum(ti * hb - 1, 0), j + off))
    st = lambda off: pl.BlockSpec((None, SUBLANES, tc), lambda bi, ti, j: (bi, 0, j + off))
    wsp = lambda off: pl.BlockSpec((FFN_CONV, tc), lambda bi, ti, j: (0, j + off))
    bsp = lambda off: pl.BlockSpec((1, tc), lambda bi, ti, j: (0, j + off))
    return pl.pallas_call(
        _ffn_gate_kernel,
        grid=(b, t // tt, nj),
        in_specs=[cur(0), cur(nj), halo(0), halo(nj), st(0), st(nj), wsp(0), wsp(nj),
                  bsp(0), bsp(nj)],
        out_specs=pl.BlockSpec((None, tt, tc), lambda bi, ti, j: (bi, ti, j)),
        out_shape=jax.ShapeDtypeStruct((b, t, dff), BF16),
        scratch_shapes=[pltpu.VMEM((SUBLANES + tt, tc), F32),
                        pltpu.VMEM((SUBLANES + tt, tc), F32)],
        compiler_params=_cparams(3),
        name="ffn_gate",
    )(u, u, u, u, state8, state8, conv_w, conv_w, conv_b.reshape(1, c2), conv_b.reshape(1, c2))


def _ffn_up_fused_kernel(h_ref, w1_ref, w2_ref, s1_ref, s2_ref, cw1_ref, cw2_ref, cb1_ref, cb2_ref,
                         g_ref, t1_ref, t2_ref, cat1_ref, cat2_ref, carry_ref, *, tiles_per_seq):
    i = pl.program_id(0)
    j = pl.program_id(1)
    nj = pl.num_programs(1)
    tm = h_ref.shape[0]

    @pl.when((i == 0) & (j == 0))
    def _():
        carry_ref[...] = jnp.zeros_like(carry_ref)

    first = (i % tiles_per_seq) == 0
    a = h_ref[...]

    def half(w_ref, s_ref, cw_ref, cb_ref, cat_ref, tail_ref, slot):
        acc = jnp.dot(a, w_ref[...], preferred_element_type=F32)
        cat_ref[0:SUBLANES, :] = jnp.where(first, s_ref[...], carry_ref[slot])
        cat_ref[SUBLANES:SUBLANES + tm, :] = acc
        tail = acc[tm - SUBLANES:tm, :]
        carry_ref[slot] = tail
        tail_ref[...] = tail
        w = cw_ref[...]
        y = cat_ref[pl.ds(SUBLANES - 2, tm), :] * w[0:1, :] + cb_ref[...]
        y = y + cat_ref[pl.ds(SUBLANES - 1, tm), :] * w[1:2, :]
        return y + acc * w[2:3, :]

    y1 = half(w1_ref, s1_ref, cw1_ref, cb1_ref, cat1_ref, t1_ref, j)
    y2 = half(w2_ref, s2_ref, cw2_ref, cb2_ref, cat2_ref, t2_ref, nj + j)
    g_ref[...] = (_silu(y1) * y2).astype(g_ref.dtype)


def ffn_up_fused(h, w_up, state8, conv_w, conv_b, *, b, t, tm, tn):
    m, k = h.shape
    c2 = w_up.shape[1]
    dff = c2 // 2
    nj = dff // tn
    assert t % tm == 0 and m == b * t and tm % SUBLANES == 0
    tps = t // tm
    wsp = lambda off: pl.BlockSpec((k, tn), lambda i, j: (0, j + off))
    ssp = lambda off: pl.BlockSpec((None, SUBLANES, tn), lambda i, j: (i // tps, 0, j + off))
    cwsp = lambda off: pl.BlockSpec((FFN_CONV, tn), lambda i, j: (0, j + off))
    cbsp = lambda off: pl.BlockSpec((1, tn), lambda i, j: (0, j + off))
    tail = pl.BlockSpec((None, SUBLANES, tn), lambda i, j: (i, 0, j))
    cb = conv_b.reshape(1, c2)
    g, t1, t2 = pl.pallas_call(
        functools.partial(_ffn_up_fused_kernel, tiles_per_seq=tps),
        grid=(m // tm, nj),
        in_specs=[pl.BlockSpec((tm, k), lambda i, j: (i, 0)), wsp(0), wsp(nj), ssp(0), ssp(nj),
                  cwsp(0), cwsp(nj), cbsp(0), cbsp(nj)],
        out_specs=[pl.BlockSpec((tm, tn), lambda i, j: (i, j)), tail, tail],
        out_shape=[jax.ShapeDtypeStruct((m, dff), BF16),
                   jax.ShapeDtypeStruct((m // tm, SUBLANES, dff), F32),
                   jax.ShapeDtypeStruct((m // tm, SUBLANES, dff), F32)],
        scratch_shapes=[pltpu.VMEM((SUBLANES + tm, tn), F32),
                        pltpu.VMEM((SUBLANES + tm, tn), F32),
                        pltpu.VMEM((2 * nj, SUBLANES, tn), F32)],
        compiler_params=_cparams(2),
        name="ffn_up_fused",
    )(h, w_up, w_up, state8, state8, conv_w, conv_w, cb, cb)
    return g, jnp.concatenate([t1[tps - 1::tps], t2[tps - 1::tps]], axis=-1)


def _ssd_kernel(z_ref, x_ref, xh_ref, b_ref, bh_ref, c_ref, ch_ref, dt_ref,
                sx_ref, sb_ref, sc_ref, wx_ref, wb_ref, wc_ref, bx_ref, bb_ref, bc_ref,
                dtb_ref, alog_ref, dsk_ref, ng_ref, h0_ref,
                y_ref, hl_ref,
                st_ref, catx_ref, catb_ref, catc_ref, yacc_ref, *, t_valid, n_chunks):
    g = pl.program_id(1)
    c = pl.program_id(2)
    L = SSD_CHUNK
    halo0 = SUBLANES - (SSM_CONV - 1)

    def conv_silu(cur_ref, halo_ref, state_ref, cat_ref, w_ref, bias_ref):
        @pl.when(c == 0)
        def _():
            cat_ref[0:SUBLANES, :] = state_ref[...]

        @pl.when(c > 0)
        def _():
            cat_ref[0:SUBLANES, :] = halo_ref[...]

        cat_ref[SUBLANES:SUBLANES + L, :] = cur_ref[...]
        w = w_ref[...]
        y = cat_ref[pl.ds(halo0, L), :] * w[0:1, :] + bias_ref[...]
        for k in range(1, SSM_CONV):
            y = y + cat_ref[pl.ds(halo0 + k, L), :] * w[k:k + 1, :]
        return _silu(y)

    xs = conv_silu(x_ref, xh_ref, sx_ref, catx_ref, wx_ref, bx_ref)
    bm = conv_silu(b_ref, bh_ref, sb_ref, catb_ref, wb_ref, bb_ref)
    cm = conv_silu(c_ref, ch_ref, sc_ref, catc_ref, wc_ref, bc_ref)

    @pl.when(c == 0)
    def _():
        st_ref[...] = h0_ref[...].T

    dtr = dt_ref[...] + dtb_ref[...]
    dt = jnp.maximum(dtr, 0.0) + jnp.log(1.0 + jnp.exp(-jnp.abs(dtr)))
    row = lax.broadcasted_iota(jnp.int32, (L, LANES), 0)
    dt = jnp.where(row + c * L < t_valid, dt, 0.0)
    ac = dt * (-jnp.exp(alog_ref[...]))
    s = 1
    while s < L:
        ac = ac + jnp.where(row >= s, pltpu.roll(ac, s, axis=0), 0.0)
        s *= 2
    shift = (LANES - HEADS_PER_GROUP * g) % LANES
    ac_g = pltpu.roll(ac, shift, axis=1)
    dt_g = pltpu.roll(dt, shift, axis=1)
    ac_t = ac_g.T
    dt_t = dt_g.T

    bmb = bm.astype(BF16)
    cmb = cm.astype(BF16)
    cb = lax.dot_general(cmb, bmb, (((1,), (1,)), ((), ())), preferred_element_type=F32)
    bt = bm.T
    st_prev = st_ref[...]
    yoff = jnp.dot(cmb, st_prev.astype(BF16), preferred_element_type=F32)

    col = lax.broadcasted_iota(jnp.int32, (L, L), 1)
    causal = row >= col
    lane = lax.broadcasted_iota(jnp.int32, (1, LANES), 1)
    low = lane < SSM_HEAD_DIM
    ssq = jnp.zeros((L, LANES), F32)
    for j in range(HEADS_PER_GROUP // 2):
        sl = slice(j * LANES, (j + 1) * LANES)
        xp = xs[:, sl]
        xpb = xp.astype(BF16)
        parts = []
        for e in range(2):
            h = 2 * j + e
            ac_row = ac_t[h:h + 1, :]
            dt_row = dt_t[h:h + 1, :]
            ac_col = jnp.broadcast_to(ac_g[:, h:h + 1], (L, LANES))
            ac_last = ac_col[L - 1:L, :]
            decay = jnp.exp(jnp.where(causal, ac_col - ac_row, NEG_BIG))
            mh = (cb * decay * dt_row).astype(BF16)
            ydiag = jnp.dot(mh, xpb, preferred_element_type=F32)
            w_row = dt_row * jnp.exp(ac_last - ac_row)
            btw = (bt * w_row).astype(BF16)
            stc = jnp.dot(btw, xpb, preferred_element_type=F32)
            parts.append((ydiag, stc, jnp.exp(ac_col), jnp.exp(ac_last)))
        ydiag = jnp.where(low, parts[0][0], parts[1][0])
        stc = jnp.where(low, parts[0][1], parts[1][1])
        ecol = jnp.where(low, parts[0][2], parts[1][2])
        dec = jnp.where(low, parts[0][3], parts[1][3])
        y = ydiag + ecol * yoff[:, sl] + xp * dsk_ref[:, sl]
        y = y * _silu(z_ref[:, sl])
        ssq = ssq + y * y
        yacc_ref[:, sl] = y
        st_ref[:, sl] = dec * st_prev[:, sl] + stc

    ms = jnp.sum(ssq, axis=-1, keepdims=True) * (1.0 / GROUP_WIDTH)
    rinv = lax.rsqrt(ms + NORM_EPS)
    y_ref[...] = (yacc_ref[...] * rinv * ng_ref[...]).astype(y_ref.dtype)

    @pl.when(c == n_chunks - 1)
    def _():
        hl_ref[...] = st_ref[...].T


def ssd_mixer(zx, dtraw, conv8, h0, conv_w, conv_b, dt_bias, a_log, d_skip, norm_g, *, t_valid):
    b, tp, _ = zx.shape
    nc = tp // SSD_CHUNK
    L, gw, n = SSD_CHUNK, GROUP_WIDTH, D_STATE
    hb = L // SUBLANES
    xoff = D_INNER // gw
    boff = (2 * D_INNER) // n
    coff = boff + N_SSM_GROUPS
    sboff = D_INNER // n
    scoff = sboff + N_SSM_GROUPS
    prev = lambda ci: jnp.maximum(ci * hb - 1, 0)
    in_specs = [
        pl.BlockSpec((None, L, gw), lambda bi, gi, ci: (bi, ci, gi)),
        pl.BlockSpec((None, L, gw), lambda bi, gi, ci: (bi, ci, xoff + gi)),
        pl.BlockSpec((None, SUBLANES, gw), lambda bi, gi, ci: (bi, prev(ci), xoff + gi)),
        pl.BlockSpec((None, L, n), lambda bi, gi, ci: (bi, ci, boff + gi)),
        pl.BlockSpec((None, SUBLANES, n), lambda bi, gi, ci: (bi, prev(ci), boff + gi)),
        pl.BlockSpec((None, L, n), lambda bi, gi, ci: (bi, ci, coff + gi)),
        pl.BlockSpec((None, SUBLANES, n), lambda bi, gi, ci: (bi, prev(ci), coff + gi)),
        pl.BlockSpec((None, L, N_SSM_HEADS), lambda bi, gi, ci: (bi, ci, 0)),
        pl.BlockSpec((None, SUBLANES, gw), lambda bi, gi, ci: (bi, 0, gi)),
        pl.BlockSpec((None, SUBLANES, n), lambda bi, gi, ci: (bi, 0, sboff + gi)),
        pl.BlockSpec((None, SUBLANES, n), lambda bi, gi, ci: (bi, 0, scoff + gi)),
        pl.BlockSpec((SSM_CONV, gw), lambda bi, gi, ci: (0, gi)),
        pl.BlockSpec((SSM_CONV, n), lambda bi, gi, ci: (0, sboff + gi)),
        pl.BlockSpec((SSM_CONV, n), lambda bi, gi, ci: (0, scoff + gi)),
        pl.BlockSpec((1, gw), lambda bi, gi, ci: (0, gi)),
        pl.BlockSpec((1, n), lambda bi, gi, ci: (0, sboff + gi)),
        pl.BlockSpec((1, n), lambda bi, gi, ci: (0, scoff + gi)),
        pl.BlockSpec((1, N_SSM_HEADS), lambda bi, gi, ci: (0, 0)),
        pl.BlockSpec((1, N_SSM_HEADS), lambda bi, gi, ci: (0, 0)),
        pl.BlockSpec((1, gw), lambda bi, gi, ci: (0, gi)),
        pl.BlockSpec((1, gw), lambda bi, gi, ci: (0, gi)),
        pl.BlockSpec((None, None, gw, n), lambda bi, gi, ci: (bi, gi, 0, 0)),
    ]
    d_rep = jnp.repeat(d_skip, SSM_HEAD_DIM).reshape(1, D_INNER)
    cb2 = conv_b.reshape(1, CONV_DIM)
    y, hl = pl.pallas_call(
        functools.partial(_ssd_kernel, t_valid=t_valid, n_chunks=nc),
        grid=(b, N_SSM_GROUPS, nc),
        in_specs=in_specs,
        out_specs=[pl.BlockSpec((None, L, gw), lambda bi, gi, ci: (bi, ci, gi)),
                   pl.BlockSpec((None, None, gw, n), lambda bi, gi, ci: (bi, gi, 0, 0))],
        out_shape=[jax.ShapeDtypeStruct((b, tp, D_INNER), BF16),
                   jax.ShapeDtypeStruct((b, N_SSM_GROUPS, gw, n), F32)],
        scratch_shapes=[pltpu.VMEM((n, gw), F32),
                        pltpu.VMEM((SUBLANES + L, gw), F32),
                        pltpu.VMEM((SUBLANES + L, n), F32),
                        pltpu.VMEM((SUBLANES + L, n), F32),
                        pltpu.VMEM((L, gw), F32)],
        compiler_params=_cparams(3),
        name="ssd_mixer",
    )(zx, zx, zx, zx, zx, zx, zx, dtraw, conv8, conv8, conv8, conv_w, conv_w, conv_w,
      cb2, cb2, cb2, dt_bias.reshape(1, -1), a_log.reshape(1, -1), d_rep,
      norm_g.reshape(1, D_INNER), h0.reshape(b, N_SSM_GROUPS, gw, n))
    return y, hl.reshape(b, N_SSM_HEADS, SSM_HEAD_DIM, n)


def _rel_bucket(dist):
    max_exact = N_BUCKETS // 2
    nn = np.asarray(dist, dtype=np.int64)
    ratio = np.log(np.maximum(nn, 1) / max_exact) / np.log(REL_MAX_DIST / max_exact)
    large = np.minimum(max_exact + (np.maximum(ratio, 0.0) * (N_BUCKETS - max_exact)).astype(np.int64),
                       N_BUCKETS - 1)
    return np.where(nn < max_exact, nn, large).astype(np.int32)


def _bias_by_step(rel_bias, g):
    dists = np.arange(Q_BLOCK + 1) * DIL_RATES[g]
    return rel_bias[_rel_bucket(dists)][:, g * N_SLOTS:(g + 1) * N_SLOTS].astype(F32)


def _toeplitz_bias(rel_bias):
    ql = np.arange(Q_BLOCK)[:, None]
    kl = np.arange(2 * Q_BLOCK)[None, :]
    step = ql + Q_BLOCK - kl
    ok = (step >= 0) & (step <= Q_BLOCK)
    idx = np.clip(step, 0, Q_BLOCK)
    tiles = []
    for g in range(N_DIL_GROUPS):
        tab = _bias_by_step(rel_bias, g)
        t = jnp.where(ok[None], jnp.transpose(tab[idx], (2, 0, 1)), NEG_BIG)
        tiles.append(t)
    return jnp.stack(tiles)


def _attn_prompt_kernel(q0_ref, q1_ref, q2_ref, k_ref, v_ref, bias_ref, o_ref,
                        qf_ref, acc_ref, m_ref, l_ref, *, t):
    q_refs = (q0_ref, q1_ref, q2_ref)
    col = lax.broadcasted_iota(jnp.int32, (KV_REP * Q_BLOCK, 2 * Q_BLOCK), 1)
    for g in range(N_DIL_GROUPS):
        rate = DIL_RATES[g]
        span = rate * Q_BLOCK
        nqb = t // span
        for r in range(KV_REP):
            qf_ref[r] = q_refs[g][:, r * HEAD_DIM:(r + 1) * HEAD_DIM].astype(F32)

        def rows_at(start, rate=rate):
            return pl.ds(start, Q_BLOCK) if rate == 1 else pl.ds(start, Q_BLOCK, stride=rate)

        def body(idx, carry, g=g, span=span, nqb=nqb, rows_at=rows_at):
            rho = idx // nqb
            qb = idx - rho * nqb
            start = rho + span * qb
            start_prev = jnp.maximum(start - span, rho)
            if span == Q_BLOCK:
                start = pl.multiple_of(start, Q_BLOCK)
                start_prev = pl.multiple_of(start_prev, Q_BLOCK)
            rows = rows_at(start)
            rows_prev = rows_at(start_prev)
            qs = jnp.concatenate([qf_ref[r, rows, :] for r in range(KV_REP)], axis=0).astype(BF16)
            k2 = jnp.concatenate([k_ref[rows_prev, :], k_ref[rows, :]], axis=0).astype(BF16)
            v2 = jnp.concatenate([v_ref[rows_prev, :], v_ref[rows, :]], axis=0).astype(BF16)
            s = lax.dot_general(qs, k2, (((1,), (1,)), ((), ())), preferred_element_type=F32)
            s = s + bias_ref[g].reshape(KV_REP * Q_BLOCK, 2 * Q_BLOCK)
            s = jnp.where((col >= Q_BLOCK) | (qb > 0), s, NEG_BIG)
            mg = jnp.max(s, axis=-1, keepdims=True)
            p = jnp.exp(s - mg)
            lg = jnp.sum(p, axis=-1, keepdims=True)
            og = jnp.dot(p.astype(BF16), v2, preferred_element_type=F32)
            for r in range(KV_REP):
                rs = slice(r * Q_BLOCK, (r + 1) * Q_BLOCK)
                o_r = og[rs]
                m_r = jnp.broadcast_to(mg[rs], (Q_BLOCK, HEAD_DIM))
                l_r = jnp.broadcast_to(lg[rs], (Q_BLOCK, HEAD_DIM))
                if g > 0:
                    m_old = m_ref[r, rows, :]
                    m_new = jnp.maximum(m_old, m_r)
                    a_old = jnp.exp(m_old - m_new)
                    a_new = jnp.exp(m_r - m_new)
                    o_r = acc_ref[r, rows, :] * a_old + o_r * a_new
                    l_r = l_ref[r, rows, :] * a_old + l_r * a_new
                    m_r = m_new
                if g == N_DIL_GROUPS - 1:
                    acc_ref[r, rows, :] = o_r / l_r
                else:
                    acc_ref[r, rows, :] = o_r
                    m_ref[r, rows, :] = m_r
                    l_ref[r, rows, :] = l_r
            return carry

        lax.fori_loop(0, rate * nqb, body, 0)
    for r in range(KV_REP):
        o_ref[:, r * HEAD_DIM:(r + 1) * HEAD_DIM] = acc_ref[r].astype(o_ref.dtype)


def attention_prompt(q, k, v, bias_t, b, t):
    rw = KV_REP * HEAD_DIM
    qspec = lambda g: pl.BlockSpec((t, rw), lambda bi, hi: (bi, g * N_KV_HEADS + hi))
    kvspec = pl.BlockSpec((None, t, HEAD_DIM), lambda bi, hi: (bi, 0, hi))
    return pl.pallas_call(
        functools.partial(_attn_prompt_kernel, t=t),
        grid=(b, N_KV_HEADS),
        in_specs=[qspec(0), qspec(1), qspec(2), kvspec, kvspec,
                  pl.BlockSpec((N_DIL_GROUPS, KV_REP, Q_BLOCK, 2 * Q_BLOCK), lambda bi, hi: (0, hi, 0, 0))],
        out_specs=pl.BlockSpec((t, rw), lambda bi, hi: (bi, hi)),
        out_shape=jax.ShapeDtypeStruct((b * t, N_SLOTS * HEAD_DIM), BF16),
        scratch_shapes=[pltpu.VMEM((KV_REP, t, HEAD_DIM), F32)] * 4,
        compiler_params=_cparams(2),
        name="attn_prompt",
    )(q, q, q, k, v, bias_t)


def _attn_sample_kernel(q_ref, kc_ref, kn_ref, vc_ref, vn_ref, bm_ref, bs_ref, o_ref,
                        kbuf_ref, vbuf_ref, *, t_new):
    w = WINDOW_MAX
    rows = N_DIL_GROUPS * t_new * KV_REP
    kbuf_ref[0:w, :] = kc_ref[...]
    kbuf_ref[w:w + SUBLANES, :] = kn_ref[...]
    vbuf_ref[0:w, :] = vc_ref[...]
    vbuf_ref[w:w + SUBLANES, :] = vn_ref[...]
    q = q_ref[...]
    qf = q.astype(F32)
    ridx = lax.broadcasted_iota(jnp.int32, (rows, 1), 0)

    def row_mask(g, j):
        base = (g * t_new + j) * KV_REP
        return (ridx >= base) & (ridx < base + KV_REP)

    kself = jnp.zeros((rows, HEAD_DIM), F32)
    vself = jnp.zeros((rows, HEAD_DIM), F32)
    for j in range(t_new):
        tok_j = row_mask(0, j)
        for g in range(1, N_DIL_GROUPS):
            tok_j = tok_j | row_mask(g, j)
        kself = jnp.where(tok_j, kn_ref[j:j + 1, :].astype(BF16).astype(F32), kself)
        vself = jnp.where(tok_j, vn_ref[j:j + 1, :], vself)
    s_self = jnp.sum(qf * kself, axis=-1, keepdims=True) + bs_ref[:, 0:1]
    s = jnp.zeros((rows, Q_BLOCK), F32)
    for g in range(N_DIL_GROUPS):
        for j in range(t_new):
            start = w + j - Q_BLOCK * DIL_RATES[g]
            ks = kbuf_ref[pl.ds(start, Q_BLOCK, stride=DIL_RATES[g]), :].astype(BF16)
            sg = lax.dot_general(q, ks, (((1,), (1,)), ((), ())), preferred_element_type=F32)
            s = jnp.where(row_mask(g, j), sg, s)
    s = s + bm_ref[...]
    m = jnp.maximum(jnp.max(s, axis=-1, keepdims=True), s_self)
    p = jnp.exp(s - m)
    p_self = jnp.exp(s_self - m)
    l = jnp.sum(p, axis=-1, keepdims=True) + p_self
    pb = p.astype(BF16)
    acc = p_self * vself
    for g in range(N_DIL_GROUPS):
        for j in range(t_new):
            start = w + j - Q_BLOCK * DIL_RATES[g]
            vs = vbuf_ref[pl.ds(start, Q_BLOCK, stride=DIL_RATES[g]), :].astype(BF16)
            og = jnp.dot(pb, vs, preferred_element_type=F32)
            acc = acc + jnp.where(row_mask(g, j), og, 0.0)
    o = acc / l
    lse = m + jnp.log(l)
    n = t_new * KV_REP
    ls = [lse[g * n:(g + 1) * n] for g in range(N_DIL_GROUPS)]
    mm = jnp.maximum(jnp.maximum(ls[0], ls[1]), ls[2])
    es = [jnp.exp(x - mm) for x in ls]
    inv = 1.0 / (es[0] + es[1] + es[2])
    out = sum((es[g] * inv) * o[g * n:(g + 1) * n] for g in range(N_DIL_GROUPS))
    o_ref[...] = out.astype(o_ref.dtype)


def attention_sample(q, k_new, v_new, k_cache, v_cache, rel_bias, b, t_new):
    assert PAST_LEN >= WINDOW_MAX and t_new <= SUBLANES
    rows = N_DIL_GROUPS * t_new * KV_REP
    n = t_new * KV_REP
    qs = q.reshape(b, t_new, N_DIL_GROUPS, N_KV_HEADS, KV_REP, HEAD_DIM)
    qs = jnp.transpose(qs, (0, 3, 2, 1, 4, 5)).reshape(b, N_KV_HEADS, rows, HEAD_DIM)
    pad = ((0, 0), (0, SUBLANES - t_new), (0, 0))
    kn = jnp.pad(k_new, pad)
    vn = jnp.pad(v_new, pad)
    bm, bs = [], []
    for g in range(N_DIL_GROUPS):
        tab = _bias_by_step(rel_bias, g)
        tab = tab.reshape(Q_BLOCK + 1, N_KV_HEADS, KV_REP)
        main = jnp.transpose(tab[1:][::-1], (1, 2, 0))
        bm.append(jnp.broadcast_to(main[:, None], (N_KV_HEADS, t_new, KV_REP, Q_BLOCK)))
        bs.append(jnp.broadcast_to(tab[0][:, None, :, None], (N_KV_HEADS, t_new, KV_REP, LANES)))
    bm = jnp.stack(bm, axis=1).reshape(N_KV_HEADS, rows, Q_BLOCK)
    bs = jnp.stack(bs, axis=1).reshape(N_KV_HEADS, rows, LANES)
    cache = pl.BlockSpec((None, WINDOW_MAX, HEAD_DIM), lambda bi, hi: (bi, 0, hi))
    new = pl.BlockSpec((None, SUBLANES, HEAD_DIM), lambda bi, hi: (bi, 0, hi))
    o = pl.pallas_call(
        functools.partial(_attn_sample_kernel, t_new=t_new),
        grid=(b, N_KV_HEADS),
        in_specs=[pl.BlockSpec((None, None, rows, HEAD_DIM), lambda bi, hi: (bi, hi, 0, 0)),
                  cache, new, cache, new,
                  pl.BlockSpec((None, rows, Q_BLOCK), lambda bi, hi: (hi, 0, 0)),
                  pl.BlockSpec((None, rows, LANES), lambda bi, hi: (hi, 0, 0))],
        out_specs=pl.BlockSpec((None, None, n, HEAD_DIM), lambda bi, hi: (bi, hi, 0, 0)),
        out_shape=jax.ShapeDtypeStruct((b, N_KV_HEADS, n, HEAD_DIM), BF16),
        scratch_shapes=[pltpu.VMEM((WINDOW_MAX + SUBLANES, HEAD_DIM), F32),
                        pltpu.VMEM((WINDOW_MAX + SUBLANES, HEAD_DIM), F32)],
        compiler_params=_cparams(2),
        name="attn_sample",
    )(qs, k_cache, kn, v_cache, vn, bm, bs)
    o = o.reshape(b, N_KV_HEADS, t_new, KV_REP, HEAD_DIM)
    return jnp.transpose(o, (0, 2, 1, 3, 4)).reshape(b * t_new, N_SLOTS * HEAD_DIM)


def _pad_rows_front(state, rows):
    return jnp.pad(state, ((0, 0), (rows - state.shape[1], 0), (0, 0)))


def _trunk(x, p, b, t, ssm0, convs0, convf0, k_cache, v_cache, W, *, prompt):
    m = b * t
    big = m >= 1024
    tm = 1024 if big else m
    tm_k = 512 if big else m
    tnorm = 256 if big else m
    ssm_out, convs_out, convf_out = [], [], []
    k = v = None
    for i in range(DEPTH):
        if i < N_A_LAYERS:
            h = rmsnorm(x, W["norm_mix"][i], tnorm)
            zx = matmul(h, W["ssm_w_in_main"][i], tm=tm, tn=1024, name="ssm_in")
            dtraw = matmul(h, W["ssm_w_in_dt"][i], tm=tm, tn=N_SSM_HEADS, name="ssm_in_dt")
            zx = zx.reshape(b, t, XBC_END)
            dtraw = dtraw.reshape(b, t, N_SSM_HEADS)
            convs_out.append(zx[:, t - (SSM_CONV - 1):, D_INNER:])
            tp = -(-t // SSD_CHUNK) * SSD_CHUNK
            if tp != t:
                zx = jnp.pad(zx, ((0, 0), (0, tp - t), (0, 0)))
                dtraw = jnp.pad(dtraw, ((0, 0), (0, tp - t), (0, 0)))
            y, hs = ssd_mixer(zx, dtraw, _pad_rows_front(convs0[i], SUBLANES), ssm0[i],
                              W["ssm_conv_w"][i], W["ssm_conv_b"][i], W["ssm_dt_bias"][i],
                              W["ssm_a_log"][i], W["ssm_d"][i], W["ssm_norm"][i], t_valid=t)
            ssm_out.append(hs)
            y = y[:, :t].reshape(m, D_INNER)
            x = mm_residual(y, W["ssm_w_out"][i], x, tm=tm_k, tn=512, name="ssm_out")
        else:
            li = i - N_A_LAYERS
            if li == 0:
                hkv = rmsnorm(x, W["norm_kv"], tnorm)
                k = mm_headnorm(hkv, W["w_k"], W["k_gain"], tm=tm, tn=1024, out_dtype=F32, name="k_proj")
                v = matmul(hkv, W["w_v"], tm=tm, tn=1024, name="v_proj")
                k = k.reshape(b, t, N_KV_HEADS * HEAD_DIM)
                v = v.reshape(b, t, N_KV_HEADS * HEAD_DIM)
            h = rmsnorm(x, W["norm_mix"][i], tnorm)
            q = mm_headnorm(h, W["attn_w_q"][li], W["q_gain"][li], tm=tm, tn=1024, out_dtype=BF16,
                            name="q_proj")
            if prompt:
                o = attention_prompt(q, k, v, W["bias_tiles"], b, t)
            else:
                o = attention_sample(q, k, v, k_cache, v_cache, W["rel_bias"], b, t)
            x = mm_residual(o, W["attn_w_o"][li], x, tm=tm, tn=512, name="attn_out")
        h = rmsnorm(x, W["norm_ffn"][i], tnorm)
        state8 = _pad_rows_front(convf0[i], SUBLANES)
        if big and t % tm == 0:
            gte, tail = ffn_up_fused(h, W["ffn_w_up"][i], state8, W["ffn_conv_w"][i], W["ffn_conv_b"][i],
                                     b=b, t=t, tm=tm, tn=256)
            convf_out.append(tail[:, SUBLANES - (FFN_CONV - 1):])
        else:
            u = matmul(h, W["ffn_w_up"][i], tm=tm, tn=512, name="ffn_up").reshape(b, t, 2 * D_FF)
            convf_out.append(u[:, t - (FFN_CONV - 1):])
            tpad = -(-t // SUBLANES) * SUBLANES
            if tpad != t:
                u = jnp.pad(u, ((0, 0), (0, tpad - t), (0, 0)))
            gte = ffn_gate(u, state8, W["ffn_conv_w"][i], W["ffn_conv_b"][i], tt=min(512, tpad), tc=256)
            gte = gte[:, :t].reshape(m, D_FF)
        x = mm_residual(gte, W["ffn_w_down"][i], x, tm=tm_k, tn=256, name="ffn_down")
        h = rmsnorm(x, W["norm_ple"][i], tnorm)
        x = mm_ple(h, W["ple_w_gate"][i], p[i], W["ple_w_proj"][i], x, tm=tm, tn=512)
    return x, jnp.stack(ssm_out), jnp.stack(convs_out), jnp.stack(convf_out), k, v


def kernel(x_prompt, x_sample, p_prompt, p_sample, state_ssm, state_conv_ssm, state_conv_ffn, cache_k_window, cache_v_window, norm_mix, norm_ffn, norm_ple, ssm_w_in, ssm_conv_w, ssm_conv_b, ssm_dt_bias, ssm_a_log, ssm_d, ssm_norm, ssm_w_out, norm_kv, w_k, w_v, k_norm, attn_w_q, q_norm, attn_w_o, rel_bias, ffn_w_up, ffn_conv_w, ffn_conv_b, ffn_w_down, ple_w_proj, ple_w_gate):
    bp, tp, _ = x_prompt.shape
    bs, ts, _ = x_sample.shape
    assert tp % (Q_BLOCK * max(DIL_RATES)) == 0 and tp <= WINDOW_MAX
    assert cache_k_window.shape[1] == WINDOW_MAX
    q_gain = jnp.broadcast_to(q_norm[:, :, None, :], (DEPTH - N_A_LAYERS, N_DIL_GROUPS, N_SLOTS, HEAD_DIM))
    W = dict(
        norm_mix=norm_mix, norm_ffn=norm_ffn, norm_ple=norm_ple, norm_kv=norm_kv,
        ssm_w_in_main=ssm_w_in[:, :, :XBC_END].astype(BF16),
        ssm_w_in_dt=ssm_w_in[:, :, XBC_END:].astype(BF16),
        ssm_conv_w=ssm_conv_w, ssm_conv_b=ssm_conv_b, ssm_dt_bias=ssm_dt_bias, ssm_a_log=ssm_a_log,
        ssm_d=ssm_d, ssm_norm=ssm_norm, ssm_w_out=ssm_w_out.astype(BF16),
        w_k=w_k.astype(BF16), w_v=w_v.astype(BF16),
        k_gain=jnp.tile(k_norm, N_KV_HEADS).reshape(1, -1),
        attn_w_q=attn_w_q.astype(BF16),
        q_gain=(q_gain * (HEAD_DIM ** -0.5)).reshape(DEPTH - N_A_LAYERS, 1, -1),
        attn_w_o=attn_w_o.astype(BF16), rel_bias=rel_bias, bias_tiles=_toeplitz_bias(rel_bias),
        ffn_w_up=ffn_w_up.astype(BF16), ffn_conv_w=ffn_conv_w, ffn_conv_b=ffn_conv_b,
        ffn_w_down=ffn_w_down.astype(BF16), ple_w_proj=ple_w_proj.astype(BF16),
        ple_w_gate=ple_w_gate.astype(BF16))

    f = lambda a: jnp.zeros(a.shape[:1] + (bp,) + a.shape[2:], a.dtype)
    y_p, ssm_p, convs_p, convf_p, k_p, v_p = _trunk(
        x_prompt.reshape(bp * tp, D_MODEL), p_prompt.reshape(DEPTH, bp * tp, PLE_DIM).astype(BF16),
        bp, tp, f(state_ssm), f(state_conv_ssm), f(state_conv_ffn), None, None, W, prompt=True)
    kc = cache_k_window.reshape(bs, WINDOW_MAX, N_KV_HEADS * HEAD_DIM)
    vc = cache_v_window.reshape(bs, WINDOW_MAX, N_KV_HEADS * HEAD_DIM)
    y_s, ssm_s, convs_s, convf_s, k_s, v_s = _trunk(
        x_sample.reshape(bs * ts, D_MODEL), p_sample.reshape(DEPTH, bs * ts, PLE_DIM).astype(BF16),
        bs, ts, state_ssm, state_conv_ssm, state_conv_ffn, kc, vc, W, prompt=False)
    kv4 = lambda a, b_, t_: a.reshape(b_, t_, N_KV_HEADS, HEAD_DIM)
    return (y_p.reshape(bp, tp, D_MODEL), y_s.reshape(bs, ts, D_MODEL), ssm_p, ssm_s, convs_p, convs_s,
            convf_p, convf_s, kv4(k_p, bp, tp), kv4(v_p, bp, tp), kv4(k_s, bs, ts), kv4(v_s, bs, ts))
```

```python
import functools

import numpy as np
import jax
import jax.numpy as jnp
from jax import lax
from jax.experimental import pallas as pl
from jax.experimental.pallas import tpu as pltpu

F32 = jnp.float32
BF16 = jnp.bfloat16

D_MODEL = 4096
DEPTH = 4
N_A_LAYERS = DEPTH // 2
PAST_LEN = 8192
D_INNER = 2 * D_MODEL
SSM_HEAD_DIM = 64
N_SSM_HEADS = D_INNER // SSM_HEAD_DIM
D_STATE = 128
N_SSM_GROUPS = 8
HEADS_PER_GROUP = N_SSM_HEADS // N_SSM_GROUPS
GROUP_WIDTH = D_INNER // N_SSM_GROUPS
SSM_CONV = 4
SSD_CHUNK = 128
CONV_DIM = D_INNER + 2 * N_SSM_GROUPS * D_STATE
XBC_END = D_INNER + CONV_DIM
HEAD_DIM = 128
N_SLOTS = D_MODEL // HEAD_DIM
N_KV_HEADS = 8
KV_REP = N_SLOTS // N_KV_HEADS
DIL_WINDOWS = (128, 512, 2048)
DIL_RATES = (1, 4, 16)
N_DIL_GROUPS = len(DIL_WINDOWS)
WINDOW_MAX = max(DIL_WINDOWS)
Q_BLOCK = 128
N_BUCKETS = 32
REL_MAX_DIST = WINDOW_MAX
D_FF = 256 * ((8 * D_MODEL // 3 + 255) // 256)
FFN_CONV = 3
PLE_DIM = 256
NORM_EPS = 1e-6

LANES = 128
SUBLANES = 8
VMEM_LIMIT_BYTES = 56 * 1024 * 1024
NEG_BIG = -1e30


def _cparams(n_axes):
    return pltpu.CompilerParams(dimension_semantics=("arbitrary",) * n_axes,
                                vmem_limit_bytes=VMEM_LIMIT_BYTES)


def _sigmoid(x):
    return 1.0 / (1.0 + jnp.exp(-x))


def _silu(x):
    return x * _sigmoid(x)


def _rmsnorm_kernel(x_ref, g_ref, o_ref):
    x = x_ref[...]
    ms = jnp.mean(x * x, axis=-1, keepdims=True)
    o_ref[...] = (x * lax.rsqrt(ms + NORM_EPS) * g_ref[...]).astype(o_ref.dtype)


def rmsnorm(x, gain, tm):
    m, d = x.shape
    return pl.pallas_call(
        _rmsnorm_kernel,
        grid=(m // tm,),
        in_specs=[pl.BlockSpec((tm, d), lambda i: (i, 0)),
                  pl.BlockSpec((1, d), lambda i: (0, 0))],
        out_specs=pl.BlockSpec((tm, d), lambda i: (i, 0)),
        out_shape=jax.ShapeDtypeStruct((m, d), BF16),
        compiler_params=_cparams(1),
        name="rmsnorm",
    )(x, gain.reshape(1, d))


def _headnorm(acc, gain):
    cols = []
    for c in range(acc.shape[1] // HEAD_DIM):
        blk = acc[:, c * HEAD_DIM:(c + 1) * HEAD_DIM]
        ms = jnp.mean(blk * blk, axis=-1, keepdims=True)
        cols.append(blk * lax.rsqrt(ms + NORM_EPS) * gain[:, c * HEAD_DIM:(c + 1) * HEAD_DIM])
    return jnp.concatenate(cols, axis=1)


def _mm_res_kernel(a_ref, w_ref, res_ref, o_ref):
    o_ref[...] = res_ref[...] + jnp.dot(a_ref[...], w_ref[...], preferred_element_type=F32)


def mm_residual(a, w, layer, res, *, tm, tn, name):
    m, k = a.shape
    n = w.shape[2]
    assert m % tm == 0 and n % tn == 0, (m, tm, n, tn)
    return pl.pallas_call(
        _mm_res_kernel,
        grid=(m // tm, n // tn),
        in_specs=[pl.BlockSpec((tm, k), lambda i, j: (i, 0)),
                  pl.BlockSpec((None, k, tn), lambda i, j: (layer, 0, j)),
                  pl.BlockSpec((tm, tn), lambda i, j: (i, j))],
        out_specs=pl.BlockSpec((tm, tn), lambda i, j: (i, j)),
        out_shape=jax.ShapeDtypeStruct((m, n), F32),
        compiler_params=_cparams(2),
        name=name,
    )(a, w, res)


def _ws_epilogue(kind, acc, ins, aux, o_ref):
    if kind == "plain":
        o_ref[...] = acc.astype(o_ref.dtype)
    elif kind == "residual":
        o_ref[...] = ins[0][...] + acc
    elif kind == "headnorm":
        o_ref[...] = _headnorm(acc, ins[0][...]).astype(o_ref.dtype)
    else:
        proj = jnp.dot(ins[0][...], aux[...], preferred_element_type=F32)
        o_ref[...] = ins[1][...] + _sigmoid(acc) * proj


def _ws_kernel(*refs, kind):
    ap_ref, as_ref, w_ref = refs[:3]
    n_extra = {"plain": 0, "residual": 1, "headnorm": 1, "ple": 2}[kind]
    pos = 3
    aux_ref = None
    if kind == "ple":
        aux_ref = refs[pos]
        pos += 1
    if kind == "headnorm":
        ins_p = ins_s = refs[pos:pos + 1]
        pos += 1
    else:
        ins_p = refs[pos:pos + n_extra]
        ins_s = refs[pos + n_extra:pos + 2 * n_extra]
        pos += 2 * n_extra
    op_ref, os_ref, wb_ref = refs[pos:pos + 3]
    auxb_ref = refs[pos + 3] if kind == "ple" else None

    @pl.when(pl.program_id(1) == 0)
    def _():
        wb_ref[...] = w_ref[...].astype(BF16)
        if kind == "ple":
            auxb_ref[...] = aux_ref[...].astype(BF16)
        acc_s = jnp.dot(as_ref[...], wb_ref[...], preferred_element_type=F32)
        _ws_epilogue(kind, acc_s, ins_s, auxb_ref, os_ref)

    acc = jnp.dot(ap_ref[...], wb_ref[...], preferred_element_type=F32)
    _ws_epilogue(kind, acc, ins_p, auxb_ref, op_ref)


def ws_matmul(a_p, a_s, w, layer, *, n_cols, col0=0, tm, tn, kind="plain", extras_p=(), extras_s=(),
              aux=None, out_dtype=F32, name):
    mp, k = a_p.shape
    ms = a_s.shape[0]
    assert mp % tm == 0 and n_cols % tn == 0 and col0 % tn == 0
    nj, ni, c0 = n_cols // tn, mp // tm, col0 // tn
    in_specs = [pl.BlockSpec((tm, k), lambda j, i: (i, 0)),
                pl.BlockSpec((ms, k), lambda j, i: (0, 0)),
                pl.BlockSpec((None, k, tn), lambda j, i: (layer, 0, c0 + j))]
    args = [a_p, a_s, w]
    scratch = [pltpu.VMEM((k, tn), BF16)]
    tile_p = pl.BlockSpec((tm, tn), lambda j, i: (i, j))
    tile_s = pl.BlockSpec((ms, tn), lambda j, i: (0, j))
    if kind == "ple":
        pd = aux.shape[1]
        in_specs.append(pl.BlockSpec((None, pd, tn), lambda j, i: (layer, 0, c0 + j)))
        args.append(aux)
        scratch.append(pltpu.VMEM((pd, tn), BF16))
        in_specs += [pl.BlockSpec((tm, pd), lambda j, i: (i, 0)), tile_p,
                     pl.BlockSpec((ms, pd), lambda j, i: (0, 0)), tile_s]
    elif kind == "residual":
        in_specs += [tile_p, tile_s]
    elif kind == "headnorm":
        in_specs.append(pl.BlockSpec((1, tn), lambda j, i: (0, j)))
    args += list(extras_p) + list(extras_s)
    return pl.pallas_call(
        functools.partial(_ws_kernel, kind=kind),
        grid=(nj, ni),
        in_specs=in_specs,
        out_specs=[tile_p, tile_s],
        out_shape=[jax.ShapeDtypeStruct((mp, n_cols), out_dtype),
                   jax.ShapeDtypeStruct((ms, n_cols), out_dtype)],
        scratch_shapes=scratch,
        compiler_params=_cparams(2),
        name=name,
    )(*args)


def _ffn_gate_kernel(u1_ref, u2_ref, h1_ref, h2_ref, s1_ref, s2_ref, w1_ref, w2_ref,
                     b1_ref, b2_ref, o_ref, cat1_ref, cat2_ref):
    tb = pl.program_id(1)
    tt = u1_ref.shape[0]

    def conv(u_ref, halo_ref, state_ref, w_ref, b_ref, cat_ref):
        @pl.when(tb == 0)
        def _():
            cat_ref[0:SUBLANES, :] = state_ref[...]

        @pl.when(tb > 0)
        def _():
            cat_ref[0:SUBLANES, :] = halo_ref[...]

        cat_ref[SUBLANES:SUBLANES + tt, :] = u_ref[...]
        w = w_ref[...]
        y = cat_ref[pl.ds(SUBLANES - 2, tt), :] * w[0:1, :] + b_ref[...]
        y = y + cat_ref[pl.ds(SUBLANES - 1, tt), :] * w[1:2, :]
        y = y + cat_ref[pl.ds(SUBLANES, tt), :] * w[2:3, :]
        return y

    y1 = conv(u1_ref, h1_ref, s1_ref, w1_ref, b1_ref, cat1_ref)
    y2 = conv(u2_ref, h2_ref, s2_ref, w2_ref, b2_ref, cat2_ref)
    o_ref[...] = (_silu(y1) * y2).astype(o_ref.dtype)


def ffn_gate(u, state8, conv_w, conv_b, *, tt, tc):
    b, t, c2 = u.shape
    dff = c2 // 2
    nj = dff // tc
    hb = tt // SUBLANES
    cur = lambda off: pl.BlockSpec((None, tt, tc), lambda bi, ti, j: (bi, ti, j + off))
    halo = lambda off: pl.BlockSpec(
        (None, SUBLANES, tc), lambda bi, ti, j: (bi, jnp.maximum(ti * hb - 1, 0), j + off))
    st = lambda off: pl.BlockSpec((None, SUBLANES, tc), lambda bi, ti, j: (bi, 0, j + off))
    wsp = lambda off: pl.BlockSpec((FFN_CONV, tc), lambda bi, ti, j: (0, j + off))
    bsp = lambda off: pl.BlockSpec((1, tc), lambda bi, ti, j: (0, j + off))
    return pl.pallas_call(
        _ffn_gate_kernel,
        grid=(b, t // tt, nj),
        in_specs=[cur(0), cur(nj), halo(0), halo(nj), st(0), st(nj), wsp(0), wsp(nj),
                  bsp(0), bsp(nj)],
        out_specs=pl.BlockSpec((None, tt, tc), lambda bi, ti, j: (bi, ti, j)),
        out_shape=jax.ShapeDtypeStruct((b, t, dff), BF16),
        scratch_shapes=[pltpu.VMEM((SUBLANES + tt, tc), F32),
                        pltpu.VMEM((SUBLANES + tt, tc), F32)],
        compiler_params=_cparams(3),
        name="ffn_gate",
    )(u, u, u, u, state8, state8, conv_w, conv_w, conv_b.reshape(1, c2), conv_b.reshape(1, c2))


def _ffn_up_fused_kernel(h_ref, hs_ref, w1_ref, w2_ref, s1_ref, s2_ref, cw1_ref, cw2_ref, cb1_ref, cb2_ref,
                         g_ref, t1_ref, t2_ref, us1_ref, us2_ref,
                         wb1_ref, wb2_ref, cat1_ref, cat2_ref, carry_ref, *, tiles_per_seq):
    j = pl.program_id(0)
    i = pl.program_id(1)
    tm = h_ref.shape[0]

    @pl.when((i == 0) & (j == 0))
    def _():
        carry_ref[...] = jnp.zeros_like(carry_ref)

    @pl.when(i == 0)
    def _():
        wb1_ref[...] = w1_ref[...].astype(BF16)
        wb2_ref[...] = w2_ref[...].astype(BF16)
        hs = hs_ref[...]
        us1_ref[...] = jnp.dot(hs, wb1_ref[...], preferred_element_type=F32)
        us2_ref[...] = jnp.dot(hs, wb2_ref[...], preferred_element_type=F32)

    first = (i % tiles_per_seq) == 0
    a = h_ref[...]

    def half(wb_ref, s_ref, cw_ref, cb_ref, cat_ref, tail_ref, slot):
        acc = jnp.dot(a, wb_ref[...], preferred_element_type=F32)
        cat_ref[0:SUBLANES, :] = jnp.where(first, s_ref[...], carry_ref[slot])
        cat_ref[SUBLANES:SUBLANES + tm, :] = acc
        tail = acc[tm - SUBLANES:tm, :]
        carry_ref[slot] = tail
        tail_ref[...] = tail
        w = cw_ref[...]
        y = cat_ref[pl.ds(SUBLANES - 2, tm), :] * w[0:1, :] + cb_ref[...]
        y = y + cat_ref[pl.ds(SUBLANES - 1, tm), :] * w[1:2, :]
        return y + acc * w[2:3, :]

    y1 = half(wb1_ref, s1_ref, cw1_ref, cb1_ref, cat1_ref, t1_ref, 0)
    y2 = half(wb2_ref, s2_ref, cw2_ref, cb2_ref, cat2_ref, t2_ref, 1)
    g_ref[...] = (_silu(y1) * y2).astype(g_ref.dtype)


def ffn_up_fused(h, h_s, w_up, layer, state8, conv_w, conv_b, *, b, t, tm, tn):
    m, k = h.shape
    ms = h_s.shape[0]
    c2 = w_up.shape[2]
    dff = c2 // 2
    nj = dff // tn
    assert t % tm == 0 and m == b * t and tm % SUBLANES == 0
    tps = t // tm
    wsp = lambda off: pl.BlockSpec((None, k, tn), lambda j, i: (layer, 0, j + off))
    ssp = lambda off: pl.BlockSpec((None, SUBLANES, tn), lambda j, i: (i // tps, 0, j + off))
    cwsp = lambda off: pl.BlockSpec((FFN_CONV, tn), lambda j, i: (0, j + off))
    cbsp = lambda off: pl.BlockSpec((1, tn), lambda j, i: (0, j + off))
    tail = pl.BlockSpec((None, SUBLANES, tn), lambda j, i: (i, 0, j))
    usp = pl.BlockSpec((ms, tn), lambda j, i: (0, j))
    cb = conv_b.reshape(1, c2)
    g, t1, t2, us1, us2 = pl.pallas_call(
        functools.partial(_ffn_up_fused_kernel, tiles_per_seq=tps),
        grid=(nj, m // tm),
        in_specs=[pl.BlockSpec((tm, k), lambda j, i: (i, 0)), pl.BlockSpec((ms, k), lambda j, i: (0, 0)),
                  wsp(0), wsp(nj), ssp(0), ssp(nj), cwsp(0), cwsp(nj), cbsp(0), cbsp(nj)],
        out_specs=[pl.BlockSpec((tm, tn), lambda j, i: (i, j)), tail, tail, usp, usp],
        out_shape=[jax.ShapeDtypeStruct((m, dff), BF16),
                   jax.ShapeDtypeStruct((m // tm, SUBLANES, dff), F32),
                   jax.ShapeDtypeStruct((m // tm, SUBLANES, dff), F32),
                   jax.ShapeDtypeStruct((ms, dff), F32),
                   jax.ShapeDtypeStruct((ms, dff), F32)],
        scratch_shapes=[pltpu.VMEM((k, tn), BF16),
                        pltpu.VMEM((k, tn), BF16),
                        pltpu.VMEM((SUBLANES + tm, tn), F32),
                        pltpu.VMEM((SUBLANES + tm, tn), F32),
                        pltpu.VMEM((2, SUBLANES, tn), F32)],
        compiler_params=_cparams(2),
        name="ffn_up_fused",
    )(h, h_s, w_up, w_up, state8, state8, conv_w, conv_w, cb, cb)
    tails = jnp.concatenate([t1[tps - 1::tps], t2[tps - 1::tps]], axis=-1)
    return g, tails, jnp.concatenate([us1, us2], axis=-1)


def _ssd_kernel(z_ref, x_ref, xh_ref, b_ref, bh_ref, c_ref, ch_ref, dt_ref,
                sx_ref, sb_ref, sc_ref, wx_ref, wb_ref, wc_ref, bx_ref, bb_ref, bc_ref,
                dtb_ref, alog_ref, dsk_ref, ng_ref, h0_ref,
                y_ref, hl_ref,
                st_ref, catx_ref, catb_ref, catc_ref, yacc_ref, *, t_valid, n_chunks):
    g = pl.program_id(1)
    c = pl.program_id(2)
    L = SSD_CHUNK
    halo0 = SUBLANES - (SSM_CONV - 1)

    def conv_silu(cur_ref, halo_ref, state_ref, cat_ref, w_ref, bias_ref):
        @pl.when(c == 0)
        def _():
            cat_ref[0:SUBLANES, :] = state_ref[...]

        @pl.when(c > 0)
        def _():
            cat_ref[0:SUBLANES, :] = halo_ref[...]

        cat_ref[SUBLANES:SUBLANES + L, :] = cur_ref[...]
        w = w_ref[...]
        y = cat_ref[pl.ds(halo0, L), :] * w[0:1, :] + bias_ref[...]
        for k in range(1, SSM_CONV):
            y = y + cat_ref[pl.ds(halo0 + k, L), :] * w[k:k + 1, :]
        return _silu(y)

    xs = conv_silu(x_ref, xh_ref, sx_ref, catx_ref, wx_ref, bx_ref)
    bm = conv_silu(b_ref, bh_ref, sb_ref, catb_ref, wb_ref, bb_ref)
    cm = conv_silu(c_ref, ch_ref, sc_ref, catc_ref, wc_ref, bc_ref)

    @pl.when(c == 0)
    def _():
        st_ref[...] = h0_ref[...].T

    dtr = dt_ref[...] + dtb_ref[...]
    dt = jnp.maximum(dtr, 0.0) + jnp.log(1.0 + jnp.exp(-jnp.abs(dtr)))
    row = lax.broadcasted_iota(jnp.int32, (L, LANES), 0)
    dt = jnp.where(row + c * L < t_valid, dt, 0.0)
    ac = dt * (-jnp.exp(alog_ref[...]))
    s = 1
    while s < L:
        ac = ac + jnp.where(row >= s, pltpu.roll(ac, s, axis=0), 0.0)
        s *= 2
    shift = (LANES - HEADS_PER_GROUP * g) % LANES
    ac_g = pltpu.roll(ac, shift, axis=1)
    dt_g = pltpu.roll(dt, shift, axis=1)
    ac_t = ac_g.T
    dt_t = dt_g.T

    bmb = bm.astype(BF16)
    cmb = cm.astype(BF16)
    cb = lax.dot_general(cmb, bmb, (((1,), (1,)), ((), ())), preferred_element_type=F32)
    bt = bm.T
    st_prev = st_ref[...]
    yoff = jnp.dot(cmb, st_prev.astype(BF16), preferred_element_type=F32)

    col = lax.broadcasted_iota(jnp.int32, (L, L), 1)
    causal = row >= col
    lane = lax.broadcasted_iota(jnp.int32, (1, LANES), 1)
    low = lane < SSM_HEAD_DIM
    ssq = jnp.zeros((L, LANES), F32)
    for j in range(HEADS_PER_GROUP // 2):
        sl = slice(j * LANES, (j + 1) * LANES)
        xp = xs[:, sl]
        xpb = xp.astype(BF16)
        parts = []
        for e in range(2):
            h = 2 * j + e
            ac_row = ac_t[h:h + 1, :]
            dt_row = dt_t[h:h + 1, :]
            ac_col = jnp.broadcast_to(ac_g[:, h:h + 1], (L, LANES))
            ac_last = ac_col[L - 1:L, :]
            decay = jnp.exp(jnp.where(causal, ac_col - ac_row, NEG_BIG))
            mh = (cb * decay * dt_row).astype(BF16)
            ydiag = jnp.dot(mh, xpb, preferred_element_type=F32)
            w_row = dt_row * jnp.exp(ac_last - ac_row)
            btw = (bt * w_row).astype(BF16)
            stc = jnp.dot(btw, xpb, preferred_element_type=F32)
            parts.append((ydiag, stc, jnp.exp(ac_col), jnp.exp(ac_last)))
        ydiag = jnp.where(low, parts[0][0], parts[1][0])
        stc = jnp.where(low, parts[0][1], parts[1][1])
        ecol = jnp.where(low, parts[0][2], parts[1][2])
        dec = jnp.where(low, parts[0][3], parts[1][3])
        y = ydiag + ecol * yoff[:, sl] + xp * dsk_ref[:, sl]
        y = y * _silu(z_ref[:, sl])
        ssq = ssq + y * y
        yacc_ref[:, sl] = y
        st_ref[:, sl] = dec * st_prev[:, sl] + stc

    ms = jnp.sum(ssq, axis=-1, keepdims=True) * (1.0 / GROUP_WIDTH)
    rinv = lax.rsqrt(ms + NORM_EPS)
    y_ref[...] = (yacc_ref[...] * rinv * ng_ref[...]).astype(y_ref.dtype)

    @pl.when(c == n_chunks - 1)
    def _():
        hl_ref[...] = st_ref[...].T


def ssd_mixer(zx, dtraw, conv8, h0, conv_w, conv_b, dt_bias, a_log, d_skip, norm_g, *, t_valid):
    b, tp, _ = zx.shape
    nc = tp // SSD_CHUNK
    L, gw, n = SSD_CHUNK, GROUP_WIDTH, D_STATE
    hb = L // SUBLANES
    xoff = D_INNER // gw
    boff = (2 * D_INNER) // n
    coff = boff + N_SSM_GROUPS
    sboff = D_INNER // n
    scoff = sboff + N_SSM_GROUPS
    prev = lambda ci: jnp.maximum(ci * hb - 1, 0)
    in_specs = [
        pl.BlockSpec((None, L, gw), lambda bi, gi, ci: (bi, ci, gi)),
        pl.BlockSpec((None, L, gw), lambda bi, gi, ci: (bi, ci, xoff + gi)),
        pl.BlockSpec((None, SUBLANES, gw), lambda bi, gi, ci: (bi, prev(ci), xoff + gi)),
        pl.BlockSpec((None, L, n), lambda bi, gi, ci: (bi, ci, boff + gi)),
        pl.BlockSpec((None, SUBLANES, n), lambda bi, gi, ci: (bi, prev(ci), boff + gi)),
        pl.BlockSpec((None, L, n), lambda bi, gi, ci: (bi, ci, coff + gi)),
        pl.BlockSpec((None, SUBLANES, n), lambda bi, gi, ci: (bi, prev(ci), coff + gi)),
        pl.BlockSpec((None, L, N_SSM_HEADS), lambda bi, gi, ci: (bi, ci, 0)),
        pl.BlockSpec((None, SUBLANES, gw), lambda bi, gi, ci: (bi, 0, gi)),
        pl.BlockSpec((None, SUBLANES, n), lambda bi, gi, ci: (bi, 0, sboff + gi)),
        pl.BlockSpec((None, SUBLANES, n), lambda bi, gi, ci: (bi, 0, scoff + gi)),
        pl.BlockSpec((SSM_CONV, gw), lambda bi, gi, ci: (0, gi)),
        pl.BlockSpec((SSM_CONV, n), lambda bi, gi, ci: (0, sboff + gi)),
        pl.BlockSpec((SSM_CONV, n), lambda bi, gi, ci: (0, scoff + gi)),
        pl.BlockSpec((1, gw), lambda bi, gi, ci: (0, gi)),
        pl.BlockSpec((1, n), lambda bi, gi, ci: (0, sboff + gi)),
        pl.BlockSpec((1, n), lambda bi, gi, ci: (0, scoff + gi)),
        pl.BlockSpec((1, N_SSM_HEADS), lambda bi, gi, ci: (0, 0)),
        pl.BlockSpec((1, N_SSM_HEADS), lambda bi, gi, ci: (0, 0)),
        pl.BlockSpec((1, gw), lambda bi, gi, ci: (0, gi)),
        pl.BlockSpec((1, gw), lambda bi, gi, ci: (0, gi)),
        pl.BlockSpec((None, None, gw, n), lambda bi, gi, ci: (bi, gi, 0, 0)),
    ]
    d_rep = jnp.repeat(d_skip, SSM_HEAD_DIM).reshape(1, D_INNER)
    cb2 = conv_b.reshape(1, CONV_DIM)
    y, hl = pl.pallas_call(
        functools.partial(_ssd_kernel, t_valid=t_valid, n_chunks=nc),
        grid=(b, N_SSM_GROUPS, nc),
        in_specs=in_specs,
        out_specs=[pl.BlockSpec((None, L, gw), lambda bi, gi, ci: (bi, ci, gi)),
                   pl.BlockSpec((None, None, gw, n), lambda bi, gi, ci: (bi, gi, 0, 0))],
        out_shape=[jax.ShapeDtypeStruct((b, tp, D_INNER), BF16),
                   jax.ShapeDtypeStruct((b, N_SSM_GROUPS, gw, n), F32)],
        scratch_shapes=[pltpu.VMEM((n, gw), F32),
                        pltpu.VMEM((SUBLANES + L, gw), F32),
                        pltpu.VMEM((SUBLANES + L, n), F32),
                        pltpu.VMEM((SUBLANES + L, n), F32),
                        pltpu.VMEM((L, gw), F32)],
        compiler_params=_cparams(3),
        name="ssd_mixer",
    )(zx, zx, zx, zx, zx, zx, zx, dtraw, conv8, conv8, conv8, conv_w, conv_w, conv_w,
      cb2, cb2, cb2, dt_bias.reshape(1, -1), a_log.reshape(1, -1), d_rep,
      norm_g.reshape(1, D_INNER), h0.reshape(b, N_SSM_GROUPS, gw, n))
    return y, hl.reshape(b, N_SSM_HEADS, SSM_HEAD_DIM, n)


def _rel_bucket(dist):
    max_exact = N_BUCKETS // 2
    nn = np.asarray(dist, dtype=np.int64)
    ratio = np.log(np.maximum(nn, 1) / max_exact) / np.log(REL_MAX_DIST / max_exact)
    large = np.minimum(max_exact + (np.maximum(ratio, 0.0) * (N_BUCKETS - max_exact)).astype(np.int64),
                       N_BUCKETS - 1)
    return np.where(nn < max_exact, nn, large).astype(np.int32)


def _bias_by_step(rel_bias, g):
    dists = np.arange(Q_BLOCK + 1) * DIL_RATES[g]
    return rel_bias[_rel_bucket(dists)][:, g * N_SLOTS:(g + 1) * N_SLOTS].astype(F32)


def _toeplitz_bias(rel_bias):
    ql = np.arange(Q_BLOCK)[:, None]
    kl = np.arange(2 * Q_BLOCK)[None, :]
    step = ql + Q_BLOCK - kl
    ok = (step >= 0) & (step <= Q_BLOCK)
    idx = np.clip(step, 0, Q_BLOCK)
    tiles = []
    for g in range(N_DIL_GROUPS):
        tab = _bias_by_step(rel_bias, g)
        t = jnp.where(ok[None], jnp.transpose(tab[idx], (2, 0, 1)), NEG_BIG)
        tiles.append(t)
    return jnp.stack(tiles)


def _attn_prompt_kernel(q0_ref, q1_ref, q2_ref, k_ref, v_ref, bias_ref, o_ref,
                        qf_ref, acc_ref, m_ref, l_ref, *, t):
    q_refs = (q0_ref, q1_ref, q2_ref)
    col = lax.broadcasted_iota(jnp.int32, (KV_REP * Q_BLOCK, 2 * Q_BLOCK), 1)
    for g in range(N_DIL_GROUPS):
        rate = DIL_RATES[g]
        span = rate * Q_BLOCK
        nqb = t // span
        for r in range(KV_REP):
            qf_ref[r] = q_refs[g][:, r * HEAD_DIM:(r + 1) * HEAD_DIM].astype(F32)

        def rows_at(start, rate=rate):
            return pl.ds(start, Q_BLOCK) if rate == 1 else pl.ds(start, Q_BLOCK, stride=rate)

        def body(idx, carry, g=g, span=span, nqb=nqb, rows_at=rows_at):
            rho = idx // nqb
            qb = idx - rho * nqb
            start = rho + span * qb
            start_prev = jnp.maximum(start - span, rho)
            if span == Q_BLOCK:
                start = pl.multiple_of(start, Q_BLOCK)
                start_prev = pl.multiple_of(start_prev, Q_BLOCK)
            rows = rows_at(start)
            rows_prev = rows_at(start_prev)
            qs = jnp.concatenate([qf_ref[r, rows, :] for r in range(KV_REP)], axis=0).astype(BF16)
            k2 = jnp.concatenate([k_ref[rows_prev, :], k_ref[rows, :]], axis=0).astype(BF16)
            v2 = jnp.concatenate([v_ref[rows_prev, :], v_ref[rows, :]], axis=0).astype(BF16)
            s = lax.dot_general(qs, k2, (((1,), (1,)), ((), ())), preferred_element_type=F32)
            s = s + bias_ref[g].reshape(KV_REP * Q_BLOCK, 2 * Q_BLOCK)
            s = jnp.where((col >= Q_BLOCK) | (qb > 0), s, NEG_BIG)
            mg = jnp.max(s, axis=-1, keepdims=True)
            p = jnp.exp(s - mg)
            lg = jnp.sum(p, axis=-1, keepdims=True)
            og = jnp.dot(p.astype(BF16), v2, preferred_element_type=F32)
            for r in range(KV_REP):
                rs = slice(r * Q_BLOCK, (r + 1) * Q_BLOCK)
                o_r = og[rs]
                m_r = jnp.broadcast_to(mg[rs], (Q_BLOCK, HEAD_DIM))
                l_r = jnp.broadcast_to(lg[rs], (Q_BLOCK, HEAD_DIM))
                if g > 0:
                    m_old = m_ref[r, rows, :]
                    m_new = jnp.maximum(m_old, m_r)
                    a_old = jnp.exp(m_old - m_new)
                    a_new = jnp.exp(m_r - m_new)
                    o_r = acc_ref[r, rows, :] * a_old + o_r * a_new
                    l_r = l_ref[r, rows, :] * a_old + l_r * a_new
                    m_r = m_new
                if g == N_DIL_GROUPS - 1:
                    acc_ref[r, rows, :] = o_r / l_r
                else:
                    acc_ref[r, rows, :] = o_r
                    m_ref[r, rows, :] = m_r
                    l_ref[r, rows, :] = l_r
            return carry

        lax.fori_loop(0, rate * nqb, body, 0)
    for r in range(KV_REP):
        o_ref[:, r * HEAD_DIM:(r + 1) * HEAD_DIM] = acc_ref[r].astype(o_ref.dtype)


def attention_prompt(q, k, v, bias_t, b, t):
    rw = KV_REP * HEAD_DIM
    qspec = lambda g: pl.BlockSpec((t, rw), lambda bi, hi: (bi, g * N_KV_HEADS + hi))
    kvspec = pl.BlockSpec((None, t, HEAD_DIM), lambda bi, hi: (bi, 0, hi))
    return pl.pallas_call(
        functools.partial(_attn_prompt_kernel, t=t),
        grid=(b, N_KV_HEADS),
        in_specs=[qspec(0), qspec(1), qspec(2), kvspec, kvspec,
                  pl.BlockSpec((N_DIL_GROUPS, KV_REP, Q_BLOCK, 2 * Q_BLOCK), lambda bi, hi: (0, hi, 0, 0))],
        out_specs=pl.BlockSpec((t, rw), lambda bi, hi: (bi, hi)),
        out_shape=jax.ShapeDtypeStruct((b * t, N_SLOTS * HEAD_DIM), BF16),
        scratch_shapes=[pltpu.VMEM((KV_REP, t, HEAD_DIM), F32)] * 4,
        compiler_params=_cparams(2),
        name="attn_prompt",
    )(q, q, q, k, v, bias_t)


def _attn_sample_kernel(q_ref, kc_ref, kn_ref, vc_ref, vn_ref, bm_ref, bs_ref, o_ref,
                        kbuf_ref, vbuf_ref, *, t_new):
    w = WINDOW_MAX
    rows = N_DIL_GROUPS * t_new * KV_REP
    kbuf_ref[0:w, :] = kc_ref[...]
    kbuf_ref[w:w + SUBLANES, :] = kn_ref[...]
    vbuf_ref[0:w, :] = vc_ref[...]
    vbuf_ref[w:w + SUBLANES, :] = vn_ref[...]
    q = q_ref[...]
    qf = q.astype(F32)
    ridx = lax.broadcasted_iota(jnp.int32, (rows, 1), 0)

    def row_mask(g, j):
        base = (g * t_new + j) * KV_REP
        return (ridx >= base) & (ridx < base + KV_REP)

    kself = jnp.zeros((rows, HEAD_DIM), F32)
    vself = jnp.zeros((rows, HEAD_DIM), F32)
    for j in range(t_new):
        tok_j = row_mask(0, j)
        for g in range(1, N_DIL_GROUPS):
            tok_j = tok_j | row_mask(g, j)
        kself = jnp.where(tok_j, kn_ref[j:j + 1, :].astype(BF16).astype(F32), kself)
        vself = jnp.where(tok_j, vn_ref[j:j + 1, :], vself)
    s_self = jnp.sum(qf * kself, axis=-1, keepdims=True) + bs_ref[:, 0:1]
    s = jnp.zeros((rows, Q_BLOCK), F32)
    for g in range(N_DIL_GROUPS):
        for j in range(t_new):
            start = w + j - Q_BLOCK * DIL_RATES[g]
            ks = kbuf_ref[pl.ds(start, Q_BLOCK, stride=DIL_RATES[g]), :].astype(BF16)
            sg = lax.dot_general(q, ks, (((1,), (1,)), ((), ())), preferred_element_type=F32)
            s = jnp.where(row_mask(g, j), sg, s)
    s = s + bm_ref[...]
    m = jnp.maximum(jnp.max(s, axis=-1, keepdims=True), s_self)
    p = jnp.exp(s - m)
    p_self = jnp.exp(s_self - m)
    l = jnp.sum(p, axis=-1, keepdims=True) + p_self
    pb = p.astype(BF16)
    acc = p_self * vself
    for g in range(N_DIL_GROUPS):
        for j in range(t_new):
            start = w + j - Q_BLOCK * DIL_RATES[g]
            vs = vbuf_ref[pl.ds(start, Q_BLOCK, stride=DIL_RATES[g]), :].astype(BF16)
            og = jnp.dot(pb, vs, preferred_element_type=F32)
            acc = acc + jnp.where(row_mask(g, j), og, 0.0)
    o = acc / l
    lse = m + jnp.log(l)
    n = t_new * KV_REP
    ls = [lse[g * n:(g + 1) * n] for g in range(N_DIL_GROUPS)]
    mm = jnp.maximum(jnp.maximum(ls[0], ls[1]), ls[2])
    es = [jnp.exp(x - mm) for x in ls]
    inv = 1.0 / (es[0] + es[1] + es[2])
    out = sum((es[g] * inv) * o[g * n:(g + 1) * n] for g in range(N_DIL_GROUPS))
    o_ref[...] = out.astype(o_ref.dtype)


def attention_sample(q, k_new, v_new, k_cache, v_cache, rel_bias, b, t_new):
    assert PAST_LEN >= WINDOW_MAX and t_new <= SUBLANES
    rows = N_DIL_GROUPS * t_new * KV_REP
    n = t_new * KV_REP
    qs = q.reshape(b, t_new, N_DIL_GROUPS, N_KV_HEADS, KV_REP, HEAD_DIM)
    qs = jnp.transpose(qs, (0, 3, 2, 1, 4, 5)).reshape(b, N_KV_HEADS, rows, HEAD_DIM)
    pad = ((0, 0), (0, SUBLANES - t_new), (0, 0))
    kn = jnp.pad(k_new, pad)
    vn = jnp.pad(v_new, pad)
    bm, bs = [], []
    for g in range(N_DIL_GROUPS):
        tab = _bias_by_step(rel_bias, g)
        tab = tab.reshape(Q_BLOCK + 1, N_KV_HEADS, KV_REP)
        main = jnp.transpose(tab[1:][::-1], (1, 2, 0))
        bm.append(jnp.broadcast_to(main[:, None], (N_KV_HEADS, t_new, KV_REP, Q_BLOCK)))
        bs.append(jnp.broadcast_to(tab[0][:, None, :, None], (N_KV_HEADS, t_new, KV_REP, LANES)))
    bm = jnp.stack(bm, axis=1).reshape(N_KV_HEADS, rows, Q_BLOCK)
    bs = jnp.stack(bs, axis=1).reshape(N_KV_HEADS, rows, LANES)
    cache = pl.BlockSpec((None, WINDOW_MAX, HEAD_DIM), lambda bi, hi: (bi, 0, hi))
    new = pl.BlockSpec((None, SUBLANES, HEAD_DIM), lambda bi, hi: (bi, 0, hi))
    o = pl.pallas_call(
        functools.partial(_attn_sample_kernel, t_new=t_new),
        grid=(b, N_KV_HEADS),
        in_specs=[pl.BlockSpec((None, None, rows, HEAD_DIM), lambda bi, hi: (bi, hi, 0, 0)),
                  cache, new, cache, new,
                  pl.BlockSpec((None, rows, Q_BLOCK), lambda bi, hi: (hi, 0, 0)),
                  pl.BlockSpec((None, rows, LANES), lambda bi, hi: (hi, 0, 0))],
        out_specs=pl.BlockSpec((None, None, n, HEAD_DIM), lambda bi, hi: (bi, hi, 0, 0)),
        out_shape=jax.ShapeDtypeStruct((b, N_KV_HEADS, n, HEAD_DIM), BF16),
        scratch_shapes=[pltpu.VMEM((WINDOW_MAX + SUBLANES, HEAD_DIM), F32),
                        pltpu.VMEM((WINDOW_MAX + SUBLANES, HEAD_DIM), F32)],
        compiler_params=_cparams(2),
        name="attn_sample",
    )(qs, k_cache, kn, v_cache, vn, bm, bs)
    o = o.reshape(b, N_KV_HEADS, t_new, KV_REP, HEAD_DIM)
    return jnp.transpose(o, (0, 2, 1, 3, 4)).reshape(b * t_new, N_SLOTS * HEAD_DIM)


def _pad_rows_front(state, rows):
    return jnp.pad(state, ((0, 0), (rows - state.shape[1], 0), (0, 0)))


TM_PROMPT = 1024
TN_WS = 512
TN_FFN_UP = 256
TM_LONGK = 512
TN_SSM_OUT = 512
TN_FFN_DOWN = 256
TM_NORM = 256
TC_GATE_SAMPLE = D_FF // 2


def _ssd_block(zx, dtraw, b, t, conv0, h0, layer, W):
    zx = zx.reshape(b, t, XBC_END)
    dtraw = dtraw.reshape(b, t, N_SSM_HEADS)
    conv_new = zx[:, t - (SSM_CONV - 1):, D_INNER:]
    tp = -(-t // SSD_CHUNK) * SSD_CHUNK
    if tp != t:
        zx = jnp.pad(zx, ((0, 0), (0, tp - t), (0, 0)))
        dtraw = jnp.pad(dtraw, ((0, 0), (0, tp - t), (0, 0)))
    y, hs = ssd_mixer(zx, dtraw, _pad_rows_front(conv0, SUBLANES), h0,
                      W["ssm_conv_w"][layer], W["ssm_conv_b"][layer], W["ssm_dt_bias"][layer],
                      W["ssm_a_log"][layer], W["ssm_d"][layer], W["ssm_norm"][layer], t_valid=t)
    return y[:, :t].reshape(b * t, D_INNER), hs, conv_new


def _trunk(xp, xs, pp, ps, bp, tp, bs, ts, ssm0_s, convs0_s, convf0_s, k_cache, v_cache, W):
    ms = bs * ts
    ssm_p, ssm_s, convs_p, convs_s, convf_p, convf_s = [], [], [], [], [], []
    kp = vp = ks = vs = None
    zero_ssm = jnp.zeros((bp,) + ssm0_s.shape[2:], ssm0_s.dtype)
    zero_convs = jnp.zeros((bp,) + convs0_s.shape[2:], convs0_s.dtype)
    zero_convf8 = jnp.zeros((bp, SUBLANES, 2 * D_FF), convf0_s.dtype)
    norm2 = lambda gain: (rmsnorm(xp, gain, TM_NORM), rmsnorm(xs, gain, ms))
    ws = functools.partial(ws_matmul, tm=TM_PROMPT)
    for i in range(DEPTH):
        if i < N_A_LAYERS:
            hp, hs = norm2(W["norm_mix"][i])
            zx_p, zx_s = ws(hp, hs, W["ssm_w_in"], i, n_cols=XBC_END, tn=TN_WS, name="ssm_in")
            dt_p, dt_s = ws(hp, hs, W["ssm_w_in"], i, n_cols=N_SSM_HEADS, col0=XBC_END, tn=N_SSM_HEADS,
                            name="ssm_in_dt")
            y_p, h_p, c_p = _ssd_block(zx_p, dt_p, bp, tp, zero_convs, zero_ssm, i, W)
            y_s, h_s, c_s = _ssd_block(zx_s, dt_s, bs, ts, convs0_s[i], ssm0_s[i], i, W)
            ssm_p.append(h_p); ssm_s.append(h_s); convs_p.append(c_p); convs_s.append(c_s)
            xp = mm_residual(y_p, W["ssm_w_out"], i, xp, tm=TM_LONGK, tn=TN_SSM_OUT, name="ssm_out")
            xs = mm_residual(y_s, W["ssm_w_out"], i, xs, tm=ms, tn=TN_SSM_OUT, name="ssm_out")
        else:
            li = i - N_A_LAYERS
            if li == 0:
                hp, hs = norm2(W["norm_kv"])
                kp, ks = ws(hp, hs, W["w_k"], 0, n_cols=N_KV_HEADS * HEAD_DIM, tn=TN_WS, kind="headnorm",
                            extras_p=(W["k_gain"],), name="k_proj")
                vp, vs = ws(hp, hs, W["w_v"], 0, n_cols=N_KV_HEADS * HEAD_DIM, tn=TN_WS, name="v_proj")
                kp, vp = (a.reshape(bp, tp, N_KV_HEADS * HEAD_DIM) for a in (kp, vp))
                ks, vs = (a.reshape(bs, ts, N_KV_HEADS * HEAD_DIM) for a in (ks, vs))
            hp, hs = norm2(W["norm_mix"][i])
            qp, qs = ws(hp, hs, W["attn_w_q"], li, n_cols=N_DIL_GROUPS * D_MODEL, tn=TN_WS, kind="headnorm",
                        extras_p=(W["q_gain"][li],), out_dtype=BF16, name="q_proj")
            op = attention_prompt(qp, kp, vp, W["bias_tiles"], bp, tp)
            os_ = attention_sample(qs, ks, vs, k_cache, v_cache, W["rel_bias"], bs, ts)
            xp, xs = ws(op, os_, W["attn_w_o"], li, n_cols=D_MODEL, tn=TN_WS, kind="residual",
                        extras_p=(xp,), extras_s=(xs,), name="attn_out")
        hp, hs = norm2(W["norm_ffn"][i])
        g_p, tails, u_s = ffn_up_fused(hp, hs, W["ffn_w_up"], i, zero_convf8, W["ffn_conv_w"][i],
                                       W["ffn_conv_b"][i], b=bp, t=tp, tm=TM_PROMPT, tn=TN_FFN_UP)
        convf_p.append(tails[:, SUBLANES - (FFN_CONV - 1):])
        u_s = u_s.reshape(bs, ts, 2 * D_FF)
        convf_s.append(u_s[:, ts - (FFN_CONV - 1):])
        tpad = -(-ts // SUBLANES) * SUBLANES
        u_s = jnp.pad(u_s, ((0, 0), (0, tpad - ts), (0, 0)))
        g_s = ffn_gate(u_s, _pad_rows_front(convf0_s[i], SUBLANES), W["ffn_conv_w"][i], W["ffn_conv_b"][i],
                       tt=tpad, tc=TC_GATE_SAMPLE)[:, :ts].reshape(ms, D_FF)
        xp = mm_residual(g_p, W["ffn_w_down"], i, xp, tm=TM_LONGK, tn=TN_FFN_DOWN, name="ffn_down")
        xs = mm_residual(g_s, W["ffn_w_down"], i, xs, tm=ms, tn=TN_FFN_DOWN, name="ffn_down")
        hp, hs = norm2(W["norm_ple"][i])
        xp, xs = ws(hp, hs, W["ple_w_gate"], i, n_cols=D_MODEL, tn=TN_WS, kind="ple",
                    extras_p=(pp[i], xp), extras_s=(ps[i], xs), aux=W["ple_w_proj"], name="ple_gate")
    st = jnp.stack
    return (xp, xs, st(ssm_p), st(ssm_s), st(convs_p), st(convs_s), st(convf_p), st(convf_s), kp, vp, ks, vs)


def kernel(x_prompt, x_sample, p_prompt, p_sample, state_ssm, state_conv_ssm, state_conv_ffn, cache_k_window, cache_v_window, norm_mix, norm_ffn, norm_ple, ssm_w_in, ssm_conv_w, ssm_conv_b, ssm_dt_bias, ssm_a_log, ssm_d, ssm_norm, ssm_w_out, norm_kv, w_k, w_v, k_norm, attn_w_q, q_norm, attn_w_o, rel_bias, ffn_w_up, ffn_conv_w, ffn_conv_b, ffn_w_down, ple_w_proj, ple_w_gate):
    bp, tp, _ = x_prompt.shape
    bs, ts, _ = x_sample.shape
    assert tp % (Q_BLOCK * max(DIL_RATES)) == 0 and tp <= WINDOW_MAX
    assert cache_k_window.shape[1] == WINDOW_MAX
    q_gain = jnp.broadcast_to(q_norm[:, :, None, :], (DEPTH - N_A_LAYERS, N_DIL_GROUPS, N_SLOTS, HEAD_DIM))
    W = dict(
        norm_mix=norm_mix, norm_ffn=norm_ffn, norm_ple=norm_ple, norm_kv=norm_kv,
        ssm_w_in=ssm_w_in, ssm_conv_w=ssm_conv_w, ssm_conv_b=ssm_conv_b, ssm_dt_bias=ssm_dt_bias,
        ssm_a_log=ssm_a_log, ssm_d=ssm_d, ssm_norm=ssm_norm, ssm_w_out=ssm_w_out.astype(BF16),
        w_k=w_k[None], w_v=w_v[None], k_gain=jnp.tile(k_norm, N_KV_HEADS).reshape(1, -1),
        attn_w_q=attn_w_q, q_gain=(q_gain * (HEAD_DIM ** -0.5)).reshape(DEPTH - N_A_LAYERS, 1, -1),
        attn_w_o=attn_w_o, rel_bias=rel_bias, bias_tiles=_toeplitz_bias(rel_bias),
        ffn_w_up=ffn_w_up, ffn_conv_w=ffn_conv_w, ffn_conv_b=ffn_conv_b,
        ffn_w_down=ffn_w_down.astype(BF16), ple_w_proj=ple_w_proj, ple_w_gate=ple_w_gate)

    kc = cache_k_window.reshape(bs, WINDOW_MAX, N_KV_HEADS * HEAD_DIM)
    vc = cache_v_window.reshape(bs, WINDOW_MAX, N_KV_HEADS * HEAD_DIM)
    (y_p, y_s, ssm_p, ssm_s, convs_p, convs_s, convf_p, convf_s, k_p, v_p, k_s, v_s) = _trunk(
        x_prompt.reshape(bp * tp, D_MODEL), x_sample.reshape(bs * ts, D_MODEL),
        p_prompt.reshape(DEPTH, bp * tp, PLE_DIM).astype(BF16),
        p_sample.reshape(DEPTH, bs * ts, PLE_DIM).astype(BF16),
        bp, tp, bs, ts, state_ssm, state_conv_ssm, state_conv_ffn, kc, vc, W)
    kv4 = lambda a, b_, t_: a.reshape(b_, t_, N_KV_HEADS, HEAD_DIM)
    return (y_p.reshape(bp, tp, D_MODEL), y_s.reshape(bs, ts, D_MODEL), ssm_p, ssm_s, convs_p, convs_s,
            convf_p, convf_s, kv4(k_p, bp, tp), kv4(v_p, bp, tp), kv4(k_s, bs, ts), kv4(v_s, bs, ts))
```

```python
import functools

import numpy as np
import jax
import jax.numpy as jnp
from jax import lax
from jax.experimental import pallas as pl
from jax.experimental.pallas import tpu as pltpu

F32 = jnp.float32
BF16 = jnp.bfloat16

D_MODEL = 4096
DEPTH = 4
N_A_LAYERS = DEPTH // 2
PAST_LEN = 8192
D_INNER = 2 * D_MODEL
SSM_HEAD_DIM = 64
N_SSM_HEADS = D_INNER // SSM_HEAD_DIM
D_STATE = 128
N_SSM_GROUPS = 8
HEADS_PER_GROUP = N_SSM_HEADS // N_SSM_GROUPS
GROUP_WIDTH = D_INNER // N_SSM_GROUPS
SSM_CONV = 4
SSD_CHUNK = 128
CONV_DIM = D_INNER + 2 * N_SSM_GROUPS * D_STATE
XBC_END = D_INNER + CONV_DIM
HEAD_DIM = 128
N_SLOTS = D_MODEL // HEAD_DIM
N_KV_HEADS = 8
KV_REP = N_SLOTS // N_KV_HEADS
DIL_WINDOWS = (128, 512, 2048)
DIL_RATES = (1, 4, 16)
N_DIL_GROUPS = len(DIL_WINDOWS)
WINDOW_MAX = max(DIL_WINDOWS)
Q_BLOCK = 128
N_BUCKETS = 32
REL_MAX_DIST = WINDOW_MAX
D_FF = 256 * ((8 * D_MODEL // 3 + 255) // 256)
FFN_CONV = 3
PLE_DIM = 256
NORM_EPS = 1e-6

LANES = 128
SUBLANES = 8
VMEM_LIMIT_BYTES = 56 * 1024 * 1024
NEG_BIG = -1e30


def _cparams(n_axes):
    return pltpu.CompilerParams(dimension_semantics=("arbitrary",) * n_axes,
                                vmem_limit_bytes=VMEM_LIMIT_BYTES)


def _sigmoid(x):
    return 0.5 * jnp.tanh(0.5 * x) + 0.5


def _silu(x):
    h = 0.5 * x
    return h * jnp.tanh(h) + h


def _rmsnorm_kernel(x_ref, g_ref, o_ref):
    x = x_ref[...]
    ms = jnp.mean(x * x, axis=-1, keepdims=True)
    o_ref[...] = (x * lax.rsqrt(ms + NORM_EPS) * g_ref[...]).astype(o_ref.dtype)


def rmsnorm(x, gain, tm):
    m, d = x.shape
    return pl.pallas_call(
        _rmsnorm_kernel,
        grid=(m // tm,),
        in_specs=[pl.BlockSpec((tm, d), lambda i: (i, 0)),
                  pl.BlockSpec((1, d), lambda i: (0, 0))],
        out_specs=pl.BlockSpec((tm, d), lambda i: (i, 0)),
        out_shape=jax.ShapeDtypeStruct((m, d), BF16),
        compiler_params=_cparams(1),
        name="rmsnorm",
    )(x, gain.reshape(1, d))


def _headnorm(acc, gain):
    cols = []
    for c in range(acc.shape[1] // HEAD_DIM):
        blk = acc[:, c * HEAD_DIM:(c + 1) * HEAD_DIM]
        ms = jnp.mean(blk * blk, axis=-1, keepdims=True)
        cols.append(blk * lax.rsqrt(ms + NORM_EPS) * gain[:, c * HEAD_DIM:(c + 1) * HEAD_DIM])
    return jnp.concatenate(cols, axis=1)


def _mm_res_kernel(a_ref, w_ref, res_ref, o_ref):
    o_ref[...] = res_ref[...] + jnp.dot(a_ref[...], w_ref[...], preferred_element_type=F32)


def mm_residual(a, w, layer, res, *, tm, tn, name):
    m, k = a.shape
    n = w.shape[2]
    assert m % tm == 0 and n % tn == 0, (m, tm, n, tn)
    return pl.pallas_call(
        _mm_res_kernel,
        grid=(m // tm, n // tn),
        in_specs=[pl.BlockSpec((tm, k), lambda i, j: (i, 0)),
                  pl.BlockSpec((None, k, tn), lambda i, j: (layer, 0, j)),
                  pl.BlockSpec((tm, tn), lambda i, j: (i, j))],
        out_specs=pl.BlockSpec((tm, tn), lambda i, j: (i, j)),
        out_shape=jax.ShapeDtypeStruct((m, n), F32),
        compiler_params=_cparams(2),
        name=name,
    )(a, w, res)


def _ws_epilogue(kind, acc, ins, aux, o_ref):
    if kind == "plain":
        o_ref[...] = acc.astype(o_ref.dtype)
    elif kind == "residual":
        o_ref[...] = ins[0][...] + acc
    elif kind == "headnorm":
        o_ref[...] = _headnorm(acc, ins[0][...]).astype(o_ref.dtype)
    else:
        proj = jnp.dot(ins[0][...], aux[...], preferred_element_type=F32)
        o_ref[...] = ins[1][...] + _sigmoid(acc) * proj


def _ws_kernel(*refs, kind):
    ap_ref, as_ref, w_ref = refs[:3]
    n_extra = {"plain": 0, "residual": 1, "headnorm": 1, "ple": 2}[kind]
    pos = 3
    aux_ref = None
    if kind == "ple":
        aux_ref = refs[pos]
        pos += 1
    if kind == "headnorm":
        ins_p = ins_s = refs[pos:pos + 1]
        pos += 1
    else:
        ins_p = refs[pos:pos + n_extra]
        ins_s = refs[pos + n_extra:pos + 2 * n_extra]
        pos += 2 * n_extra
    op_ref, os_ref, wb_ref = refs[pos:pos + 3]
    auxb_ref = refs[pos + 3] if kind == "ple" else None

    @pl.when(pl.program_id(1) == 0)
    def _():
        wb_ref[...] = w_ref[...].astype(BF16)
        if kind == "ple":
            auxb_ref[...] = aux_ref[...].astype(BF16)
        acc_s = jnp.dot(as_ref[...], wb_ref[...], preferred_element_type=F32)
        _ws_epilogue(kind, acc_s, ins_s, auxb_ref, os_ref)

    acc = jnp.dot(ap_ref[...], wb_ref[...], preferred_element_type=F32)
    _ws_epilogue(kind, acc, ins_p, auxb_ref, op_ref)


def ws_matmul(a_p, a_s, w, layer, *, n_cols, col0=0, tm, tn, kind="plain", extras_p=(), extras_s=(),
              aux=None, out_dtype=F32, name):
    mp, k = a_p.shape
    ms = a_s.shape[0]
    assert mp % tm == 0 and n_cols % tn == 0 and col0 % tn == 0
    nj, ni, c0 = n_cols // tn, mp // tm, col0 // tn
    in_specs = [pl.BlockSpec((tm, k), lambda j, i: (i, 0)),
                pl.BlockSpec((ms, k), lambda j, i: (0, 0)),
                pl.BlockSpec((None, k, tn), lambda j, i: (layer, 0, c0 + j))]
    args = [a_p, a_s, w]
    scratch = [pltpu.VMEM((k, tn), BF16)]
    tile_p = pl.BlockSpec((tm, tn), lambda j, i: (i, j))
    tile_s = pl.BlockSpec((ms, tn), lambda j, i: (0, j))
    if kind == "ple":
        pd = aux.shape[1]
        in_specs.append(pl.BlockSpec((None, pd, tn), lambda j, i: (layer, 0, c0 + j)))
        args.append(aux)
        scratch.append(pltpu.VMEM((pd, tn), BF16))
        in_specs += [pl.BlockSpec((tm, pd), lambda j, i: (i, 0)), tile_p,
                     pl.BlockSpec((ms, pd), lambda j, i: (0, 0)), tile_s]
    elif kind == "residual":
        in_specs += [tile_p, tile_s]
    elif kind == "headnorm":
        in_specs.append(pl.BlockSpec((1, tn), lambda j, i: (0, j)))
    args += list(extras_p) + list(extras_s)
    return pl.pallas_call(
        functools.partial(_ws_kernel, kind=kind),
        grid=(nj, ni),
        in_specs=in_specs,
        out_specs=[tile_p, tile_s],
        out_shape=[jax.ShapeDtypeStruct((mp, n_cols), out_dtype),
                   jax.ShapeDtypeStruct((ms, n_cols), out_dtype)],
        scratch_shapes=scratch,
        compiler_params=_cparams(2),
        name=name,
    )(*args)


def _ffn_gate_kernel(u1_ref, u2_ref, h1_ref, h2_ref, s1_ref, s2_ref, w1_ref, w2_ref,
                     b1_ref, b2_ref, o_ref, cat1_ref, cat2_ref):
    tb = pl.program_id(1)
    tt = u1_ref.shape[0]

    def conv(u_ref, halo_ref, state_ref, w_ref, b_ref, cat_ref):
        @pl.when(tb == 0)
        def _():
            cat_ref[0:SUBLANES, :] = state_ref[...]

        @pl.when(tb > 0)
        def _():
            cat_ref[0:SUBLANES, :] = halo_ref[...]

        cat_ref[SUBLANES:SUBLANES + tt, :] = u_ref[...]
        w = w_ref[...]
        y = cat_ref[pl.ds(SUBLANES - 2, tt), :] * w[0:1, :] + b_ref[...]
        y = y + cat_ref[pl.ds(SUBLANES - 1, tt), :] * w[1:2, :]
        y = y + cat_ref[pl.ds(SUBLANES, tt), :] * w[2:3, :]
        return y

    y1 = conv(u1_ref, h1_ref, s1_ref, w1_ref, b1_ref, cat1_ref)
    y2 = conv(u2_ref, h2_ref, s2_ref, w2_ref, b2_ref, cat2_ref)
    o_ref[...] = (_silu(y1) * y2).astype(o_ref.dtype)


def ffn_gate(u, state8, conv_w, conv_b, *, tt, tc):
    b, t, c2 = u.shape
    dff = c2 // 2
    nj = dff // tc
    hb = tt // SUBLANES
    cur = lambda off: pl.BlockSpec((None, tt, tc), lambda bi, ti, j: (bi, ti, j + off))
    halo = lambda off: pl.BlockSpec(
        (None, SUBLANES, tc), lambda bi, ti, j: (bi, jnp.maximum(ti * hb - 1, 0), j + off))
    st = lambda off: pl.BlockSpec((None, SUBLANES, tc), lambda bi, ti, j: (bi, 0, j + off))
    wsp = lambda off: pl.BlockSpec((FFN_CONV, tc), lambda bi, ti, j: (0, j + off))
    bsp = lambda off: pl.BlockSpec((1, tc), lambda bi, ti, j: (0, j + off))
    return pl.pallas_call(
        _ffn_gate_kernel,
        grid=(b, t // tt, nj),
        in_specs=[cur(0), cur(nj), halo(0), halo(nj), st(0), st(nj), wsp(0), wsp(nj),
                  bsp(0), bsp(nj)],
        out_specs=pl.BlockSpec((None, tt, tc), lambda bi, ti, j: (bi, ti, j)),
        out_shape=jax.ShapeDtypeStruct((b, t, dff), BF16),
        scratch_shapes=[pltpu.VMEM((SUBLANES + tt, tc), F32),
                        pltpu.VMEM((SUBLANES + tt, tc), F32)],
        compiler_params=_cparams(3),
        name="ffn_gate",
    )(u, u, u, u, state8, state8, conv_w, conv_w, conv_b.reshape(1, c2), conv_b.reshape(1, c2))


def _ffn_up_fused_kernel(h_ref, hs_ref, w1_ref, w2_ref, s1_ref, s2_ref, cw1_ref, cw2_ref, cb1_ref, cb2_ref,
                         g_ref, t1_ref, t2_ref, us1_ref, us2_ref,
                         wb1_ref, wb2_ref, cat1_ref, cat2_ref, carry_ref, *, tiles_per_seq, n_sub):
    j = pl.program_id(0)
    i = pl.program_id(1)
    tm = h_ref.shape[0]

    @pl.when((i == 0) & (j == 0))
    def _():
        carry_ref[...] = jnp.zeros_like(carry_ref)

    @pl.when(i == 0)
    def _():
        wb1_ref[...] = w1_ref[...].astype(BF16)
        wb2_ref[...] = w2_ref[...].astype(BF16)
        hs = hs_ref[...]
        us1_ref[...] = jnp.dot(hs, wb1_ref[...], preferred_element_type=F32)
        us2_ref[...] = jnp.dot(hs, wb2_ref[...], preferred_element_type=F32)

    first = (i % tiles_per_seq) == 0
    cat1_ref[0:SUBLANES, :] = jnp.where(first, s1_ref[...], carry_ref[0])
    cat2_ref[0:SUBLANES, :] = jnp.where(first, s2_ref[...], carry_ref[1])
    sub = tm // n_sub

    def project(r0):
        a = h_ref[r0:r0 + sub, :]
        cat1_ref[SUBLANES + r0:SUBLANES + r0 + sub, :] = jnp.dot(a, wb1_ref[...], preferred_element_type=F32)
        cat2_ref[SUBLANES + r0:SUBLANES + r0 + sub, :] = jnp.dot(a, wb2_ref[...], preferred_element_type=F32)

    def conv(r0, cw_ref, cb_ref, cat_ref):
        w = cw_ref[...]
        y = cat_ref[pl.ds(SUBLANES - 2 + r0, sub), :] * w[0:1, :] + cb_ref[...]
        y = y + cat_ref[pl.ds(SUBLANES - 1 + r0, sub), :] * w[1:2, :]
        return y + cat_ref[pl.ds(SUBLANES + r0, sub), :] * w[2:3, :]

    def gate(r0):
        y1 = conv(r0, cw1_ref, cb1_ref, cat1_ref)
        y2 = conv(r0, cw2_ref, cb2_ref, cat2_ref)
        g_ref[r0:r0 + sub, :] = (_silu(y1) * y2).astype(g_ref.dtype)

    project(0)
    for r in range(1, n_sub):
        project(r * sub)
        gate((r - 1) * sub)
    gate((n_sub - 1) * sub)
    tail1 = cat1_ref[tm:tm + SUBLANES, :]
    tail2 = cat2_ref[tm:tm + SUBLANES, :]
    carry_ref[0] = tail1
    carry_ref[1] = tail2
    t1_ref[...] = tail1
    t2_ref[...] = tail2


def ffn_up_fused(h, h_s, w_up, layer, state8, conv_w, conv_b, *, b, t, tm, tn, n_sub=1):
    m, k = h.shape
    ms = h_s.shape[0]
    c2 = w_up.shape[2]
    dff = c2 // 2
    nj = dff // tn
    assert t % tm == 0 and m == b * t and tm % SUBLANES == 0
    tps = t // tm
    wsp = lambda off: pl.BlockSpec((None, k, tn), lambda j, i: (layer, 0, j + off))
    ssp = lambda off: pl.BlockSpec((None, SUBLANES, tn), lambda j, i: (i // tps, 0, j + off))
    cwsp = lambda off: pl.BlockSpec((FFN_CONV, tn), lambda j, i: (0, j + off))
    cbsp = lambda off: pl.BlockSpec((1, tn), lambda j, i: (0, j + off))
    tail = pl.BlockSpec((None, SUBLANES, tn), lambda j, i: (i, 0, j))
    usp = pl.BlockSpec((ms, tn), lambda j, i: (0, j))
    cb = conv_b.reshape(1, c2)
    g, t1, t2, us1, us2 = pl.pallas_call(
        functools.partial(_ffn_up_fused_kernel, tiles_per_seq=tps, n_sub=n_sub),
        grid=(nj, m // tm),
        in_specs=[pl.BlockSpec((tm, k), lambda j, i: (i, 0)), pl.BlockSpec((ms, k), lambda j, i: (0, 0)),
                  wsp(0), wsp(nj), ssp(0), ssp(nj), cwsp(0), cwsp(nj), cbsp(0), cbsp(nj)],
        out_specs=[pl.BlockSpec((tm, tn), lambda j, i: (i, j)), tail, tail, usp, usp],
        out_shape=[jax.ShapeDtypeStruct((m, dff), BF16),
                   jax.ShapeDtypeStruct((m // tm, SUBLANES, dff), F32),
                   jax.ShapeDtypeStruct((m // tm, SUBLANES, dff), F32),
                   jax.ShapeDtypeStruct((ms, dff), F32),
                   jax.ShapeDtypeStruct((ms, dff), F32)],
        scratch_shapes=[pltpu.VMEM((k, tn), BF16),
                        pltpu.VMEM((k, tn), BF16),
                        pltpu.VMEM((SUBLANES + tm, tn), F32),
                        pltpu.VMEM((SUBLANES + tm, tn), F32),
                        pltpu.VMEM((2, SUBLANES, tn), F32)],
        compiler_params=_cparams(2),
        name="ffn_up_fused",
    )(h, h_s, w_up, w_up, state8, state8, conv_w, conv_w, cb, cb)
    tails = jnp.concatenate([t1[tps - 1::tps], t2[tps - 1::tps]], axis=-1)
    return g, tails, jnp.concatenate([us1, us2], axis=-1)


def _ssd_kernel(z_ref, x_ref, xh_ref, b_ref, bh_ref, c_ref, ch_ref, dt_ref,
                sx_ref, sb_ref, sc_ref, wx_ref, wb_ref, wc_ref, bx_ref, bb_ref, bc_ref,
                dtb_ref, alog_ref, dsk_ref, ng_ref, h0_ref,
                y_ref, hl_ref,
                st_ref, catx_ref, catb_ref, catc_ref, yacc_ref, *, t_valid, n_chunks):
    g = pl.program_id(1)
    c = pl.program_id(2)
    L = SSD_CHUNK
    halo0 = SUBLANES - (SSM_CONV - 1)

    @pl.when(c == 0)
    def _():
        catx_ref[0:SUBLANES, :] = sx_ref[...]
        catb_ref[0:SUBLANES, :] = sb_ref[...]
        catc_ref[0:SUBLANES, :] = sc_ref[...]
        st_ref[...] = h0_ref[...].T

    @pl.when(c > 0)
    def _():
        catx_ref[0:SUBLANES, :] = xh_ref[...]
        catb_ref[0:SUBLANES, :] = bh_ref[...]
        catc_ref[0:SUBLANES, :] = ch_ref[...]

    catx_ref[SUBLANES:SUBLANES + L, :] = x_ref[...]
    catb_ref[SUBLANES:SUBLANES + L, :] = b_ref[...]
    catc_ref[SUBLANES:SUBLANES + L, :] = c_ref[...]

    def conv_silu(cat_ref, w_ref, bias_ref, sl):
        w = w_ref[:, sl]
        y = cat_ref[pl.ds(halo0, L), sl] * w[0:1, :] + bias_ref[:, sl]
        for k in range(1, SSM_CONV):
            y = y + cat_ref[pl.ds(halo0 + k, L), sl] * w[k:k + 1, :]
        return _silu(y)

    lanes0 = slice(0, LANES)
    bm = conv_silu(catb_ref, wb_ref, bb_ref, lanes0)
    cm = conv_silu(catc_ref, wc_ref, bc_ref, lanes0)

    dtr = dt_ref[...] + dtb_ref[...]
    dt = jnp.maximum(dtr, 0.0) + jnp.log(1.0 + jnp.exp(-jnp.abs(dtr)))
    row = lax.broadcasted_iota(jnp.int32, (L, LANES), 0)
    dt = jnp.where(row + c * L < t_valid, dt, 0.0)
    ac = dt * (-jnp.exp(alog_ref[...]))
    s = 1
    while s < L:
        ac = ac + jnp.where(row >= s, pltpu.roll(ac, s, axis=0), 0.0)
        s *= 2
    shift = (LANES - HEADS_PER_GROUP * g) % LANES
    ac_g = pltpu.roll(ac, shift, axis=1)
    dt_g = pltpu.roll(dt, shift, axis=1)
    ac_t = ac_g.T
    dt_t = dt_g.T

    bmb = bm.astype(BF16)
    cmb = cm.astype(BF16)
    cb = lax.dot_general(cmb, bmb, (((1,), (1,)), ((), ())), preferred_element_type=F32)
    bt = bm.T

    col = lax.broadcasted_iota(jnp.int32, (L, L), 1)
    causal = row >= col
    lane = lax.broadcasted_iota(jnp.int32, (1, LANES), 1)
    low = lane < SSM_HEAD_DIM
    ssq = jnp.zeros((L, LANES), F32)
    for j in range(HEADS_PER_GROUP // 2):
        sl = slice(j * LANES, (j + 1) * LANES)
        xp = conv_silu(catx_ref, wx_ref, bx_ref, sl)
        xpb = xp.astype(BF16)
        st_prev = st_ref[:, sl]
        yoff = jnp.dot(cmb, st_prev.astype(BF16), preferred_element_type=F32)
        parts = []
        for e in range(2):
            h = 2 * j + e
            ac_row = ac_t[h:h + 1, :]
            dt_row = dt_t[h:h + 1, :]
            ac_col = jnp.broadcast_to(ac_g[:, h:h + 1], (L, LANES))
            ac_last = ac_col[L - 1:L, :]
            decay = jnp.exp(jnp.where(causal, ac_col - ac_row, NEG_BIG))
            mh = (cb * decay * dt_row).astype(BF16)
            ydiag = jnp.dot(mh, xpb, preferred_element_type=F32)
            w_row = dt_row * jnp.exp(ac_last - ac_row)
            btw = (bt * w_row).astype(BF16)
            stc = jnp.dot(btw, xpb, preferred_element_type=F32)
            parts.append((ydiag, stc, jnp.exp(ac_col), jnp.exp(ac_last)))
        ydiag = jnp.where(low, parts[0][0], parts[1][0])
        stc = jnp.where(low, parts[0][1], parts[1][1])
        ecol = jnp.where(low, parts[0][2], parts[1][2])
        dec = jnp.where(low, parts[0][3], parts[1][3])
        y = ydiag + ecol * yoff + xp * dsk_ref[:, sl]
        y = y * _silu(z_ref[:, sl])
        ssq = ssq + y * y
        yacc_ref[:, sl] = y
        st_ref[:, sl] = dec * st_prev + stc

    ms = jnp.sum(ssq, axis=-1, keepdims=True) * (1.0 / GROUP_WIDTH)
    rinv = lax.rsqrt(ms + NORM_EPS)
    y_ref[...] = (yacc_ref[...] * rinv * ng_ref[...]).astype(y_ref.dtype)

    @pl.when(c == n_chunks - 1)
    def _():
        hl_ref[...] = st_ref[...].T


def ssd_mixer(zx, dtraw, conv8, h0, conv_w, conv_b, dt_bias, a_log, d_skip, norm_g, *, t_valid):
    b, tp, _ = zx.shape
    nc = tp // SSD_CHUNK
    L, gw, n = SSD_CHUNK, GROUP_WIDTH, D_STATE
    hb = L // SUBLANES
    xoff = D_INNER // gw
    boff = (2 * D_INNER) // n
    coff = boff + N_SSM_GROUPS
    sboff = D_INNER // n
    scoff = sboff + N_SSM_GROUPS
    prev = lambda ci: jnp.maximum(ci * hb - 1, 0)
    in_specs = [
        pl.BlockSpec((None, L, gw), lambda bi, gi, ci: (bi, ci, gi)),
        pl.BlockSpec((None, L, gw), lambda bi, gi, ci: (bi, ci, xoff + gi)),
        pl.BlockSpec((None, SUBLANES, gw), lambda bi, gi, ci: (bi, prev(ci), xoff + gi)),
        pl.BlockSpec((None, L, n), lambda bi, gi, ci: (bi, ci, boff + gi)),
        pl.BlockSpec((None, SUBLANES, n), lambda bi, gi, ci: (bi, prev(ci), boff + gi)),
        pl.BlockSpec((None, L, n), lambda bi, gi, ci: (bi, ci, coff + gi)),
        pl.BlockSpec((None, SUBLANES, n), lambda bi, gi, ci: (bi, prev(ci), coff + gi)),
        pl.BlockSpec((None, L, N_SSM_HEADS), lambda bi, gi, ci: (bi, ci, 0)),
        pl.BlockSpec((None, SUBLANES, gw), lambda bi, gi, ci: (bi, 0, gi)),
        pl.BlockSpec((None, SUBLANES, n), lambda bi, gi, ci: (bi, 0, sboff + gi)),
        pl.BlockSpec((None, SUBLANES, n), lambda bi, gi, ci: (bi, 0, scoff + gi)),
        pl.BlockSpec((SSM_CONV, gw), lambda bi, gi, ci: (0, gi)),
        pl.BlockSpec((SSM_CONV, n), lambda bi, gi, ci: (0, sboff + gi)),
        pl.BlockSpec((SSM_CONV, n), lambda bi, gi, ci: (0, scoff + gi)),
        pl.BlockSpec((1, gw), lambda bi, gi, ci: (0, gi)),
        pl.BlockSpec((1, n), lambda bi, gi, ci: (0, sboff + gi)),
        pl.BlockSpec((1, n), lambda bi, gi, ci: (0, scoff + gi)),
        pl.BlockSpec((1, N_SSM_HEADS), lambda bi, gi, ci: (0, 0)),
        pl.BlockSpec((1, N_SSM_HEADS), lambda bi, gi, ci: (0, 0)),
        pl.BlockSpec((1, gw), lambda bi, gi, ci: (0, gi)),
        pl.BlockSpec((1, gw), lambda bi, gi, ci: (0, gi)),
        pl.BlockSpec((None, None, gw, n), lambda bi, gi, ci: (bi, gi, 0, 0)),
    ]
    d_rep = jnp.repeat(d_skip, SSM_HEAD_DIM).reshape(1, D_INNER)
    cb2 = conv_b.reshape(1, CONV_DIM)
    y, hl = pl.pallas_call(
        functools.partial(_ssd_kernel, t_valid=t_valid, n_chunks=nc),
        grid=(b, N_SSM_GROUPS, nc),
        in_specs=in_specs,
        out_specs=[pl.BlockSpec((None, L, gw), lambda bi, gi, ci: (bi, ci, gi)),
                   pl.BlockSpec((None, None, gw, n), lambda bi, gi, ci: (bi, gi, 0, 0))],
        out_shape=[jax.ShapeDtypeStruct((b, tp, D_INNER), BF16),
                   jax.ShapeDtypeStruct((b, N_SSM_GROUPS, gw, n), F32)],
        scratch_shapes=[pltpu.VMEM((n, gw), F32),
                        pltpu.VMEM((SUBLANES + L, gw), F32),
                        pltpu.VMEM((SUBLANES + L, n), F32),
                        pltpu.VMEM((SUBLANES + L, n), F32),
                        pltpu.VMEM((L, gw), F32)],
        compiler_params=_cparams(3),
        name="ssd_mixer",
    )(zx, zx, zx, zx, zx, zx, zx, dtraw, conv8, conv8, conv8, conv_w, conv_w, conv_w,
      cb2, cb2, cb2, dt_bias.reshape(1, -1), a_log.reshape(1, -1), d_rep,
      norm_g.reshape(1, D_INNER), h0.reshape(b, N_SSM_GROUPS, gw, n))
    return y, hl.reshape(b, N_SSM_HEADS, SSM_HEAD_DIM, n)


def _rel_bucket(dist):
    max_exact = N_BUCKETS // 2
    nn = np.asarray(dist, dtype=np.int64)
    ratio = np.log(np.maximum(nn, 1) / max_exact) / np.log(REL_MAX_DIST / max_exact)
    large = np.minimum(max_exact + (np.maximum(ratio, 0.0) * (N_BUCKETS - max_exact)).astype(np.int64),
                       N_BUCKETS - 1)
    return np.where(nn < max_exact, nn, large).astype(np.int32)


def _bias_by_step(rel_bias, g):
    dists = np.arange(Q_BLOCK + 1) * DIL_RATES[g]
    return rel_bias[_rel_bucket(dists)][:, g * N_SLOTS:(g + 1) * N_SLOTS].astype(F32)


def _toeplitz_bias(rel_bias):
    ql = lax.broadcasted_iota(jnp.int32, (Q_BLOCK, 2 * Q_BLOCK), 0)
    kl = lax.broadcasted_iota(jnp.int32, (Q_BLOCK, 2 * Q_BLOCK), 1)
    step = ql + Q_BLOCK - kl
    ok = (step >= 0) & (step <= Q_BLOCK)
    onehot = (step[:, :, None] == jnp.arange(Q_BLOCK + 1)[None, None, :]).astype(F32)
    tabs = jnp.stack([_bias_by_step(rel_bias, g) for g in range(N_DIL_GROUPS)])
    tiles = jnp.einsum("qkx,gxs->gsqk", onehot, tabs, precision=lax.Precision.HIGHEST)
    return jnp.where(ok[None, None], tiles, NEG_BIG)


def _attn_prompt_kernel(q0_ref, q1_ref, q2_ref, k_ref, v_ref, bias_ref, o_ref,
                        qf_ref, acc_ref, m_ref, l_ref, *, t):
    q_refs = (q0_ref, q1_ref, q2_ref)
    col = lax.broadcasted_iota(jnp.int32, (KV_REP * Q_BLOCK, 2 * Q_BLOCK), 1)
    for g in range(N_DIL_GROUPS):
        rate = DIL_RATES[g]
        span = rate * Q_BLOCK
        nqb = t // span
        for r in range(KV_REP):
            qf_ref[r] = q_refs[g][:, r * HEAD_DIM:(r + 1) * HEAD_DIM].astype(F32)

        def rows_at(start, rate=rate):
            return pl.ds(start, Q_BLOCK) if rate == 1 else pl.ds(start, Q_BLOCK, stride=rate)

        def body(idx, carry, g=g, span=span, nqb=nqb, rows_at=rows_at):
            rho = idx // nqb
            qb = idx - rho * nqb
            start = rho + span * qb
            start_prev = jnp.maximum(start - span, rho)
            if span == Q_BLOCK:
                start = pl.multiple_of(start, Q_BLOCK)
                start_prev = pl.multiple_of(start_prev, Q_BLOCK)
            rows = rows_at(start)
            rows_prev = rows_at(start_prev)
            qs = jnp.concatenate([qf_ref[r, rows, :] for r in range(KV_REP)], axis=0).astype(BF16)
            k2 = jnp.concatenate([k_ref[rows_prev, :], k_ref[rows, :]], axis=0).astype(BF16)
            v2 = jnp.concatenate([v_ref[rows_prev, :], v_ref[rows, :]], axis=0).astype(BF16)
            s = lax.dot_general(qs, k2, (((1,), (1,)), ((), ())), preferred_element_type=F32)
            s = s + bias_ref[g].reshape(KV_REP * Q_BLOCK, 2 * Q_BLOCK)
            s = jnp.where((col >= Q_BLOCK) | (qb > 0), s, NEG_BIG)
            mg = jnp.max(s, axis=-1, keepdims=True)
            p = jnp.exp(s - mg)
            lg = jnp.sum(p, axis=-1, keepdims=True)
            og = jnp.dot(p.astype(BF16), v2, preferred_element_type=F32)
            for r in range(KV_REP):
                rs = slice(r * Q_BLOCK, (r + 1) * Q_BLOCK)
                o_r = og[rs]
                m_r = jnp.broadcast_to(mg[rs], (Q_BLOCK, HEAD_DIM))
                l_r = jnp.broadcast_to(lg[rs], (Q_BLOCK, HEAD_DIM))
                if g > 0:
                    m_old = m_ref[r, rows, :]
                    m_new = jnp.maximum(m_old, m_r)
                    a_old = jnp.exp(m_old - m_new)
                    a_new = jnp.exp(m_r - m_new)
                    o_r = acc_ref[r, rows, :] * a_old + o_r * a_new
                    l_r = l_ref[r, rows, :] * a_old + l_r * a_new
                    m_r = m_new
                if g == N_DIL_GROUPS - 1:
                    acc_ref[r, rows, :] = o_r / l_r
                else:
                    acc_ref[r, rows, :] = o_r
                    m_ref[r, rows, :] = m_r
                    l_ref[r, rows, :] = l_r
            return carry

        lax.fori_loop(0, rate * nqb, body, 0, unroll=4)
    for r in range(KV_REP):
        o_ref[:, r * HEAD_DIM:(r + 1) * HEAD_DIM] = acc_ref[r].astype(o_ref.dtype)


def attention_prompt(q, k, v, bias_t, b, t):
    rw = KV_REP * HEAD_DIM
    qspec = lambda g: pl.BlockSpec((t, rw), lambda bi, hi: (bi, g * N_KV_HEADS + hi))
    kvspec = pl.BlockSpec((None, t, HEAD_DIM), lambda bi, hi: (bi, 0, hi))
    return pl.pallas_call(
        functools.partial(_attn_prompt_kernel, t=t),
        grid=(b, N_KV_HEADS),
        in_specs=[qspec(0), qspec(1), qspec(2), kvspec, kvspec,
                  pl.BlockSpec((N_DIL_GROUPS, KV_REP, Q_BLOCK, 2 * Q_BLOCK), lambda bi, hi: (0, hi, 0, 0))],
        out_specs=pl.BlockSpec((t, rw), lambda bi, hi: (bi, hi)),
        out_shape=jax.ShapeDtypeStruct((b * t, N_SLOTS * HEAD_DIM), BF16),
        scratch_shapes=[pltpu.VMEM((KV_REP, t, HEAD_DIM), F32)] * 4,
        compiler_params=_cparams(2),
        name="attn_prompt",
    )(q, q, q, k, v, bias_t)


def _attn_sample_kernel(q_ref, kc_ref, kn_ref, vc_ref, vn_ref, bm_ref, bs_ref, o_ref,
                        kbuf_ref, vbuf_ref, *, t_new):
    w = WINDOW_MAX
    rows = N_DIL_GROUPS * t_new * KV_REP
    kbuf_ref[0:w, :] = kc_ref[...]
    kbuf_ref[w:w + SUBLANES, :] = kn_ref[...]
    vbuf_ref[0:w, :] = vc_ref[...]
    vbuf_ref[w:w + SUBLANES, :] = vn_ref[...]
    q = q_ref[...]
    qf = q.astype(F32)
    ridx = lax.broadcasted_iota(jnp.int32, (rows, 1), 0)

    def row_mask(g, j):
        base = (g * t_new + j) * KV_REP
        return (ridx >= base) & (ridx < base + KV_REP)

    kself = jnp.zeros((rows, HEAD_DIM), F32)
    vself = jnp.zeros((rows, HEAD_DIM), F32)
    for j in range(t_new):
        tok_j = row_mask(0, j)
        for g in range(1, N_DIL_GROUPS):
            tok_j = tok_j | row_mask(g, j)
        kself = jnp.where(tok_j, kn_ref[j:j + 1, :].astype(BF16).astype(F32), kself)
        vself = jnp.where(tok_j, vn_ref[j:j + 1, :], vself)
    s_self = jnp.sum(qf * kself, axis=-1, keepdims=True) + bs_ref[:, 0:1]
    s = jnp.zeros((rows, Q_BLOCK), F32)
    for g in range(N_DIL_GROUPS):
        for j in range(t_new):
            start = w + j - Q_BLOCK * DIL_RATES[g]
            ks = kbuf_ref[pl.ds(start, Q_BLOCK, stride=DIL_RATES[g]), :].astype(BF16)
            sg = lax.dot_general(q, ks, (((1,), (1,)), ((), ())), preferred_element_type=F32)
            s = jnp.where(row_mask(g, j), sg, s)
    s = s + bm_ref[...]
    m = jnp.maximum(jnp.max(s, axis=-1, keepdims=True), s_self)
    p = jnp.exp(s - m)
    p_self = jnp.exp(s_self - m)
    l = jnp.sum(p, axis=-1, keepdims=True) + p_self
    pb = p.astype(BF16)
    acc = p_self * vself
    for g in range(N_DIL_GROUPS):
        for j in range(t_new):
            start = w + j - Q_BLOCK * DIL_RATES[g]
            vs = vbuf_ref[pl.ds(start, Q_BLOCK, stride=DIL_RATES[g]), :].astype(BF16)
            og = jnp.dot(pb, vs, preferred_element_type=F32)
            acc = acc + jnp.where(row_mask(g, j), og, 0.0)
    o = acc / l
    lse = m + jnp.log(l)
    n = t_new * KV_REP
    ls = [lse[g * n:(g + 1) * n] for g in range(N_DIL_GROUPS)]
    mm = jnp.maximum(jnp.maximum(ls[0], ls[1]), ls[2])
    es = [jnp.exp(x - mm) for x in ls]
    inv = 1.0 / (es[0] + es[1] + es[2])
    out = sum((es[g] * inv) * o[g * n:(g + 1) * n] for g in range(N_DIL_GROUPS))
    o_ref[...] = out.astype(o_ref.dtype)


def attention_sample(q, k_new, v_new, k_cache, v_cache, rel_bias, b, t_new):
    assert PAST_LEN >= WINDOW_MAX and t_new <= SUBLANES
    rows = N_DIL_GROUPS * t_new * KV_REP
    n = t_new * KV_REP
    qs = q.reshape(b, t_new, N_DIL_GROUPS, N_KV_HEADS, KV_REP, HEAD_DIM)
    qs = jnp.transpose(qs, (0, 3, 2, 1, 4, 5)).reshape(b, N_KV_HEADS, rows, HEAD_DIM)
    pad = ((0, 0), (0, SUBLANES - t_new), (0, 0))
    kn = jnp.pad(k_new, pad)
    vn = jnp.pad(v_new, pad)
    bm, bs = [], []
    for g in range(N_DIL_GROUPS):
        tab = _bias_by_step(rel_bias, g)
        tab = tab.reshape(Q_BLOCK + 1, N_KV_HEADS, KV_REP)
        main = jnp.transpose(tab[1:][::-1], (1, 2, 0))
        bm.append(jnp.broadcast_to(main[:, None], (N_KV_HEADS, t_new, KV_REP, Q_BLOCK)))
        bs.append(jnp.broadcast_to(tab[0][:, None, :, None], (N_KV_HEADS, t_new, KV_REP, LANES)))
    bm = jnp.stack(bm, axis=1).reshape(N_KV_HEADS, rows, Q_BLOCK)
    bs = jnp.stack(bs, axis=1).reshape(N_KV_HEADS, rows, LANES)
    cache = pl.BlockSpec((None, WINDOW_MAX, HEAD_DIM), lambda bi, hi: (bi, 0, hi))
    new = pl.BlockSpec((None, SUBLANES, HEAD_DIM), lambda bi, hi: (bi, 0, hi))
    o = pl.pallas_call(
        functools.partial(_attn_sample_kernel, t_new=t_new),
        grid=(b, N_KV_HEADS),
        in_specs=[pl.BlockSpec((None, None, rows, HEAD_DIM), lambda bi, hi: (bi, hi, 0, 0)),
                  cache, new, cache, new,
                  pl.BlockSpec((None, rows, Q_BLOCK), lambda bi, hi: (hi, 0, 0)),
                  pl.BlockSpec((None, rows, LANES), lambda bi, hi: (hi, 0, 0))],
        out_specs=pl.BlockSpec((None, None, n, HEAD_DIM), lambda bi, hi: (bi, hi, 0, 0)),
        out_shape=jax.ShapeDtypeStruct((b, N_KV_HEADS, n, HEAD_DIM), BF16),
        scratch_shapes=[pltpu.VMEM((WINDOW_MAX + SUBLANES, HEAD_DIM), F32),
                        pltpu.VMEM((WINDOW_MAX + SUBLANES, HEAD_DIM), F32)],
        compiler_params=_cparams(2),
        name="attn_sample",
    )(qs, k_cache, kn, v_cache, vn, bm, bs)
    o = o.reshape(b, N_KV_HEADS, t_new, KV_REP, HEAD_DIM)
    return jnp.transpose(o, (0, 2, 1, 3, 4)).reshape(b * t_new, N_SLOTS * HEAD_DIM)


def _pad_rows_front(state, rows):
    return jnp.pad(state, ((0, 0), (rows - state.shape[1], 0), (0, 0)))


TM_PROMPT = 1024
TN_WS = 512
TN_FFN_UP = 256
TM_LONGK = 512
TN_SSM_OUT = 512
TN_FFN_DOWN = 256
TM_NORM = 256
TC_GATE_SAMPLE = D_FF // 2


def _ssd_block(zx, dtraw, b, t, conv0, h0, layer, W):
    zx = zx.reshape(b, t, XBC_END)
    dtraw = dtraw.reshape(b, t, N_SSM_HEADS)
    conv_new = zx[:, t - (SSM_CONV - 1):, D_INNER:]
    tp = -(-t // SSD_CHUNK) * SSD_CHUNK
    if tp != t:
        zx = jnp.pad(zx, ((0, 0), (0, tp - t), (0, 0)))
        dtraw = jnp.pad(dtraw, ((0, 0), (0, tp - t), (0, 0)))
    y, hs = ssd_mixer(zx, dtraw, _pad_rows_front(conv0, SUBLANES), h0,
                      W["ssm_conv_w"][layer], W["ssm_conv_b"][layer], W["ssm_dt_bias"][layer],
                      W["ssm_a_log"][layer], W["ssm_d"][layer], W["ssm_norm"][layer], t_valid=t)
    return y[:, :t].reshape(b * t, D_INNER), hs, conv_new


def _trunk(xp, xs, pp, ps, bp, tp, bs, ts, ssm0_s, convs0_s, convf0_s, k_cache, v_cache, W):
    ms = bs * ts
    ssm_p, ssm_s, convs_p, convs_s, convf_p, convf_s = [], [], [], [], [], []
    kp = vp = ks = vs = None
    zero_ssm = jnp.zeros((bp,) + ssm0_s.shape[2:], ssm0_s.dtype)
    zero_convs = jnp.zeros((bp,) + convs0_s.shape[2:], convs0_s.dtype)
    zero_convf8 = jnp.zeros((bp, SUBLANES, 2 * D_FF), convf0_s.dtype)
    norm2 = lambda gain: (rmsnorm(xp, gain, TM_NORM), rmsnorm(xs, gain, ms))
    ws = functools.partial(ws_matmul, tm=TM_PROMPT)
    for i in range(DEPTH):
        if i < N_A_LAYERS:
            hp, hs = norm2(W["norm_mix"][i])
            zx_p, zx_s = ws(hp, hs, W["ssm_w_in"], i, n_cols=XBC_END, tn=TN_WS, name="ssm_in")
            dt_p, dt_s = ws(hp, hs, W["ssm_w_in"], i, n_cols=N_SSM_HEADS, col0=XBC_END, tn=N_SSM_HEADS,
                            name="ssm_in_dt")
            y_p, h_p, c_p = _ssd_block(zx_p, dt_p, bp, tp, zero_convs, zero_ssm, i, W)
            y_s, h_s, c_s = _ssd_block(zx_s, dt_s, bs, ts, convs0_s[i], ssm0_s[i], i, W)
            ssm_p.append(h_p); ssm_s.append(h_s); convs_p.append(c_p); convs_s.append(c_s)
            xp = mm_residual(y_p, W["ssm_w_out"], i, xp, tm=TM_LONGK, tn=TN_SSM_OUT, name="ssm_out")
            xs = mm_residual(y_s, W["ssm_w_out"], i, xs, tm=ms, tn=TN_SSM_OUT, name="ssm_out")
        else:
            li = i - N_A_LAYERS
            if li == 0:
                hp, hs = norm2(W["norm_kv"])
                kp, ks = ws(hp, hs, W["w_k"], 0, n_cols=N_KV_HEADS * HEAD_DIM, tn=TN_WS, kind="headnorm",
                            extras_p=(W["k_gain"],), name="k_proj")
                vp, vs = ws(hp, hs, W["w_v"], 0, n_cols=N_KV_HEADS * HEAD_DIM, tn=TN_WS, name="v_proj")
                kp, vp = (a.reshape(bp, tp, N_KV_HEADS * HEAD_DIM) for a in (kp, vp))
                ks, vs = (a.reshape(bs, ts, N_KV_HEADS * HEAD_DIM) for a in (ks, vs))
            hp, hs = norm2(W["norm_mix"][i])
            qp, qs = ws(hp, hs, W["attn_w_q"], li, n_cols=N_DIL_GROUPS * D_MODEL, tn=TN_WS, kind="headnorm",
                        extras_p=(W["q_gain"][li],), out_dtype=BF16, name="q_proj")
            op = attention_prompt(qp, kp, vp, W["bias_tiles"], bp, tp)
            os_ = attention_sample(qs, ks, vs, k_cache, v_cache, W["rel_bias"], bs, ts)
            xp, xs = ws(op, os_, W["attn_w_o"], li, n_cols=D_MODEL, tn=TN_WS, kind="residual",
                        extras_p=(xp,), extras_s=(xs,), name="attn_out")
        hp, hs = norm2(W["norm_ffn"][i])
        g_p, tails, u_s = ffn_up_fused(hp, hs, W["ffn_w_up"], i, zero_convf8, W["ffn_conv_w"][i],
                                       W["ffn_conv_b"][i], b=bp, t=tp, tm=TM_PROMPT, tn=TN_FFN_UP)
        convf_p.append(tails[:, SUBLANES - (FFN_CONV - 1):])
        u_s = u_s.reshape(bs, ts, 2 * D_FF)
        convf_s.append(u_s[:, ts - (FFN_CONV - 1):])
        tpad = -(-ts // SUBLANES) * SUBLANES
        u_s = jnp.pad(u_s, ((0, 0), (0, tpad - ts), (0, 0)))
        g_s = ffn_gate(u_s, _pad_rows_front(convf0_s[i], SUBLANES), W["ffn_conv_w"][i], W["ffn_conv_b"][i],
                       tt=tpad, tc=TC_GATE_SAMPLE)[:, :ts].reshape(ms, D_FF)
        xp = mm_residual(g_p, W["ffn_w_down"], i, xp, tm=TM_LONGK, tn=TN_FFN_DOWN, name="ffn_down")
        xs = mm_residual(g_s, W["ffn_w_down"], i, xs, tm=ms, tn=TN_FFN_DOWN, name="ffn_down")
        hp, hs = norm2(W["norm_ple"][i])
        xp, xs = ws(hp, hs, W["ple_w_gate"], i, n_cols=D_MODEL, tn=TN_WS, kind="ple",
                    extras_p=(pp[i], xp), extras_s=(ps[i], xs), aux=W["ple_w_proj"], name="ple_gate")
    st = jnp.stack
    return (xp, xs, st(ssm_p), st(ssm_s), st(convs_p), st(convs_s), st(convf_p), st(convf_s), kp, vp, ks, vs)


def kernel(x_prompt, x_sample, p_prompt, p_sample, state_ssm, state_conv_ssm, state_conv_ffn, cache_k_window, cache_v_window, norm_mix, norm_ffn, norm_ple, ssm_w_in, ssm_conv_w, ssm_conv_b, ssm_dt_bias, ssm_a_log, ssm_d, ssm_norm, ssm_w_out, norm_kv, w_k, w_v, k_norm, attn_w_q, q_norm, attn_w_o, rel_bias, ffn_w_up, ffn_conv_w, ffn_conv_b, ffn_w_down, ple_w_proj, ple_w_gate):
    bp, tp, _ = x_prompt.shape
    bs, ts, _ = x_sample.shape
    assert tp % (Q_BLOCK * max(DIL_RATES)) == 0 and tp <= WINDOW_MAX
    assert cache_k_window.shape[1] == WINDOW_MAX
    q_gain = jnp.broadcast_to(q_norm[:, :, None, :], (DEPTH - N_A_LAYERS, N_DIL_GROUPS, N_SLOTS, HEAD_DIM))
    W = dict(
        norm_mix=norm_mix, norm_ffn=norm_ffn, norm_ple=norm_ple, norm_kv=norm_kv,
        ssm_w_in=ssm_w_in, ssm_conv_w=ssm_conv_w, ssm_conv_b=ssm_conv_b, ssm_dt_bias=ssm_dt_bias,
        ssm_a_log=ssm_a_log, ssm_d=ssm_d, ssm_norm=ssm_norm, ssm_w_out=ssm_w_out.astype(BF16),
        w_k=w_k[None], w_v=w_v[None], k_gain=jnp.tile(k_norm, N_KV_HEADS).reshape(1, -1),
        attn_w_q=attn_w_q, q_gain=(q_gain * (HEAD_DIM ** -0.5)).reshape(DEPTH - N_A_LAYERS, 1, -1),
        attn_w_o=attn_w_o, rel_bias=rel_bias, bias_tiles=_toeplitz_bias(rel_bias),
        ffn_w_up=ffn_w_up, ffn_conv_w=ffn_conv_w, ffn_conv_b=ffn_conv_b,
        ffn_w_down=ffn_w_down.astype(BF16), ple_w_proj=ple_w_proj, ple_w_gate=ple_w_gate)

    kc = cache_k_window.reshape(bs, WINDOW_MAX, N_KV_HEADS * HEAD_DIM)
    vc = cache_v_window.reshape(bs, WINDOW_MAX, N_KV_HEADS * HEAD_DIM)
    (y_p, y_s, ssm_p, ssm_s, convs_p, convs_s, convf_p, convf_s, k_p, v_p, k_s, v_s) = _trunk(
        x_prompt.reshape(bp * tp, D_MODEL), x_sample.reshape(bs * ts, D_MODEL),
        p_prompt.reshape(DEPTH, bp * tp, PLE_DIM).astype(BF16),
        p_sample.reshape(DEPTH, bs * ts, PLE_DIM).astype(BF16),
        bp, tp, bs, ts, state_ssm, state_conv_ssm, state_conv_ffn, kc, vc, W)
    kv4 = lambda a, b_, t_: a.reshape(b_, t_, N_KV_HEADS, HEAD_DIM)
    return (y_p.reshape(bp, tp, D_MODEL), y_s.reshape(bs, ts, D_MODEL), ssm_p, ssm_s, convs_p, convs_s,
            convf_p, convf_s, kv4(k_p, bp, tp), kv4(v_p, bp, tp), kv4(k_s, bs, ts), kv4(v_s, bs, ts))
```

```python
import functools

import numpy as np
import jax
import jax.numpy as jnp
from jax import lax
from jax.experimental import pallas as pl
from jax.experimental.pallas import tpu as pltpu

F32 = jnp.float32
BF16 = jnp.bfloat16

D_MODEL = 4096
DEPTH = 4
N_A_LAYERS = DEPTH // 2
PAST_LEN = 8192
D_INNER = 2 * D_MODEL
SSM_HEAD_DIM = 64
N_SSM_HEADS = D_INNER // SSM_HEAD_DIM
D_STATE = 128
N_SSM_GROUPS = 8
HEADS_PER_GROUP = N_SSM_HEADS // N_SSM_GROUPS
GROUP_WIDTH = D_INNER // N_SSM_GROUPS
SSM_CONV = 4
SSD_CHUNK = 128
SSD_CHUNKS_PER_STEP = 4
CONV_DIM =D_INNER + 2 * N_SSM_GROUPS * D_STATE
XBC_END = D_INNER + CONV_DIM
HEAD_DIM = 128
N_SLOTS = D_MODEL // HEAD_DIM
N_KV_HEADS = 8
KV_REP = N_SLOTS // N_KV_HEADS
DIL_WINDOWS = (128, 512, 2048)
DIL_RATES = (1, 4, 16)
N_DIL_GROUPS = len(DIL_WINDOWS)
WINDOW_MAX = max(DIL_WINDOWS)
Q_BLOCK = 128
N_BUCKETS = 32
REL_MAX_DIST = WINDOW_MAX
D_FF = 256 * ((8 * D_MODEL // 3 + 255) // 256)
FFN_CONV = 3
PLE_DIM = 256
NORM_EPS = 1e-6

LANES = 128
SUBLANES = 8
VMEM_LIMIT_BYTES = 56 * 1024 * 1024
NEG_BIG = -1e30


def _cparams(n_axes):
    return pltpu.CompilerParams(dimension_semantics=("arbitrary",) * n_axes,
                                vmem_limit_bytes=VMEM_LIMIT_BYTES)


def _sigmoid(x):
    return 0.5 * jnp.tanh(0.5 * x) + 0.5


def _silu(x):
    h = 0.5 * x
    return h * jnp.tanh(h) + h


def _rmsnorm_kernel(x_ref, g_ref, o_ref):
    x = x_ref[...]
    ms = jnp.mean(x * x, axis=-1, keepdims=True)
    o_ref[...] = (x * lax.rsqrt(ms + NORM_EPS) * g_ref[...]).astype(o_ref.dtype)


def rmsnorm(x, gain, tm):
    m, d = x.shape
    return pl.pallas_call(
        _rmsnorm_kernel,
        grid=(m // tm,),
        in_specs=[pl.BlockSpec((tm, d), lambda i: (i, 0)),
                  pl.BlockSpec((1, d), lambda i: (0, 0))],
        out_specs=pl.BlockSpec((tm, d), lambda i: (i, 0)),
        out_shape=jax.ShapeDtypeStruct((m, d), BF16),
        compiler_params=_cparams(1),
        name="rmsnorm",
    )(x, gain.reshape(1, d))


def _headnorm(acc, gain):
    cols = []
    for c in range(acc.shape[1] // HEAD_DIM):
        blk = acc[:, c * HEAD_DIM:(c + 1) * HEAD_DIM]
        ms = jnp.mean(blk * blk, axis=-1, keepdims=True)
        cols.append(blk * lax.rsqrt(ms + NORM_EPS) * gain[:, c * HEAD_DIM:(c + 1) * HEAD_DIM])
    return jnp.concatenate(cols, axis=1)


def _mm_res_kernel(a_ref, w_ref, res_ref, o_ref):
    o_ref[...] = res_ref[...] + jnp.dot(a_ref[...], w_ref[...], preferred_element_type=F32)


def mm_residual(a, w, layer, res, *, tm, tn, name):
    m, k = a.shape
    n = w.shape[2]
    assert m % tm == 0 and n % tn == 0, (m, tm, n, tn)
    return pl.pallas_call(
        _mm_res_kernel,
        grid=(m // tm, n // tn),
        in_specs=[pl.BlockSpec((tm, k), lambda i, j: (i, 0)),
                  pl.BlockSpec((None, k, tn), lambda i, j: (layer, 0, j)),
                  pl.BlockSpec((tm, tn), lambda i, j: (i, j))],
        out_specs=pl.BlockSpec((tm, tn), lambda i, j: (i, j)),
        out_shape=jax.ShapeDtypeStruct((m, n), F32),
        compiler_params=_cparams(2),
        name=name,
    )(a, w, res)


def _ws_epilogue(kind, acc, ins, aux, o_ref):
    if kind == "plain":
        o_ref[...] = acc.astype(o_ref.dtype)
    elif kind == "residual":
        o_ref[...] = ins[0][...] + acc
    elif kind == "headnorm":
        o_ref[...] = _headnorm(acc, ins[0][...]).astype(o_ref.dtype)
    else:
        proj = jnp.dot(ins[0][...], aux[...], preferred_element_type=F32)
        o_ref[...] = ins[1][...] + _sigmoid(acc) * proj


def _ws_kernel(*refs, kind):
    ap_ref, as_ref, w_ref = refs[:3]
    n_extra = {"plain": 0, "residual": 1, "headnorm": 1, "ple": 2}[kind]
    pos = 3
    aux_ref = None
    if kind == "ple":
        aux_ref = refs[pos]
        pos += 1
    if kind == "headnorm":
        ins_p = ins_s = refs[pos:pos + 1]
        pos += 1
    else:
        ins_p = refs[pos:pos + n_extra]
        ins_s = refs[pos + n_extra:pos + 2 * n_extra]
        pos += 2 * n_extra
    op_ref, os_ref, wb_ref = refs[pos:pos + 3]
    auxb_ref = refs[pos + 3] if kind == "ple" else None

    @pl.when(pl.program_id(1) == 0)
    def _():
        wb_ref[...] = w_ref[...].astype(BF16)
        if kind == "ple":
            auxb_ref[...] = aux_ref[...].astype(BF16)
        acc_s = jnp.dot(as_ref[...], wb_ref[...], preferred_element_type=F32)
        _ws_epilogue(kind, acc_s, ins_s, auxb_ref, os_ref)

    acc = jnp.dot(ap_ref[...], wb_ref[...], preferred_element_type=F32)
    _ws_epilogue(kind, acc, ins_p, auxb_ref, op_ref)


def ws_matmul(a_p, a_s, w, layer, *, n_cols, col0=0, tm, tn, kind="plain", extras_p=(), extras_s=(),
              aux=None, out_dtype=F32, name):
    mp, k = a_p.shape
    ms = a_s.shape[0]
    assert mp % tm == 0 and n_cols % tn == 0 and col0 % tn == 0
    nj, ni, c0 = n_cols // tn, mp // tm, col0 // tn
    in_specs = [pl.BlockSpec((tm, k), lambda j, i: (i, 0)),
                pl.BlockSpec((ms, k), lambda j, i: (0, 0)),
                pl.BlockSpec((None, k, tn), lambda j, i: (layer, 0, c0 + j))]
    args = [a_p, a_s, w]
    scratch = [pltpu.VMEM((k, tn), BF16)]
    tile_p = pl.BlockSpec((tm, tn), lambda j, i: (i, j))
    tile_s = pl.BlockSpec((ms, tn), lambda j, i: (0, j))
    if kind == "ple":
        pd = aux.shape[1]
        in_specs.append(pl.BlockSpec((None, pd, tn), lambda j, i: (layer, 0, c0 + j)))
        args.append(aux)
        scratch.append(pltpu.VMEM((pd, tn), BF16))
        in_specs += [pl.BlockSpec((tm, pd), lambda j, i: (i, 0)), tile_p,
                     pl.BlockSpec((ms, pd), lambda j, i: (0, 0)), tile_s]
    elif kind == "residual":
        in_specs += [tile_p, tile_s]
    elif kind == "headnorm":
        in_specs.append(pl.BlockSpec((1, tn), lambda j, i: (0, j)))
    args += list(extras_p) + list(extras_s)
    return pl.pallas_call(
        functools.partial(_ws_kernel, kind=kind),
        grid=(nj, ni),
        in_specs=in_specs,
        out_specs=[tile_p, tile_s],
        out_shape=[jax.ShapeDtypeStruct((mp, n_cols), out_dtype),
                   jax.ShapeDtypeStruct((ms, n_cols), out_dtype)],
        scratch_shapes=scratch,
        compiler_params=_cparams(2),
        name=name,
    )(*args)


def _ffn_gate_kernel(u1_ref, u2_ref, h1_ref, h2_ref, s1_ref, s2_ref, w1_ref, w2_ref,
                     b1_ref, b2_ref, o_ref, cat1_ref, cat2_ref):
    tb = pl.program_id(1)
    tt = u1_ref.shape[0]

    def conv(u_ref, halo_ref, state_ref, w_ref, b_ref, cat_ref):
        @pl.when(tb == 0)
        def _():
            cat_ref[0:SUBLANES, :] = state_ref[...]

        @pl.when(tb > 0)
        def _():
            cat_ref[0:SUBLANES, :] = halo_ref[...]

        cat_ref[SUBLANES:SUBLANES + tt, :] = u_ref[...]
        w = w_ref[...]
        y = cat_ref[pl.ds(SUBLANES - 2, tt), :] * w[0:1, :] + b_ref[...]
        y = y + cat_ref[pl.ds(SUBLANES - 1, tt), :] * w[1:2, :]
        y = y + cat_ref[pl.ds(SUBLANES, tt), :] * w[2:3, :]
        return y

    y1 = conv(u1_ref, h1_ref, s1_ref, w1_ref, b1_ref, cat1_ref)
    y2 = conv(u2_ref, h2_ref, s2_ref, w2_ref, b2_ref, cat2_ref)
    o_ref[...] = (_silu(y1) * y2).astype(o_ref.dtype)


def ffn_gate(u, state8, conv_w, conv_b, *, tt, tc):
    b, t, c2 = u.shape
    dff = c2 // 2
    nj = dff // tc
    hb = tt // SUBLANES
    cur = lambda off: pl.BlockSpec((None, tt, tc), lambda bi, ti, j: (bi, ti, j + off))
    halo = lambda off: pl.BlockSpec(
        (None, SUBLANES, tc), lambda bi, ti, j: (bi, jnp.maximum(ti * hb - 1, 0), j + off))
    st = lambda off: pl.BlockSpec((None, SUBLANES, tc), lambda bi, ti, j: (bi, 0, j + off))
    wsp = lambda off: pl.BlockSpec((FFN_CONV, tc), lambda bi, ti, j: (0, j + off))
    bsp = lambda off: pl.BlockSpec((1, tc), lambda bi, ti, j: (0, j + off))
    return pl.pallas_call(
        _ffn_gate_kernel,
        grid=(b, t // tt, nj),
        in_specs=[cur(0), cur(nj), halo(0), halo(nj), st(0), st(nj), wsp(0), wsp(nj),
                  bsp(0), bsp(nj)],
        out_specs=pl.BlockSpec((None, tt, tc), lambda bi, ti, j: (bi, ti, j)),
        out_shape=jax.ShapeDtypeStruct((b, t, dff), BF16),
        scratch_shapes=[pltpu.VMEM((SUBLANES + tt, tc), F32),
                        pltpu.VMEM((SUBLANES + tt, tc), F32)],
        compiler_params=_cparams(3),
        name="ffn_gate",
    )(u, u, u, u, state8, state8, conv_w, conv_w, conv_b.reshape(1, c2), conv_b.reshape(1, c2))


def _ffn_up_fused_kernel(h_ref, hs_ref, w1_ref, w2_ref, s1_ref, s2_ref, cw1_ref, cw2_ref, cb1_ref, cb2_ref,
                         g_ref, t1_ref, t2_ref, us1_ref, us2_ref,
                         wb1_ref, wb2_ref, cat1_ref, cat2_ref, carry_ref, *, tiles_per_seq, n_sub):
    j = pl.program_id(0)
    i = pl.program_id(1)
    tm = h_ref.shape[0]

    @pl.when((i == 0) & (j == 0))
    def _():
        carry_ref[...] = jnp.zeros_like(carry_ref)

    @pl.when(i == 0)
    def _():
        wb1_ref[...] = w1_ref[...].astype(BF16)
        wb2_ref[...] = w2_ref[...].astype(BF16)
        hs = hs_ref[...]
        us1_ref[...] = jnp.dot(hs, wb1_ref[...], preferred_element_type=F32)
        us2_ref[...] = jnp.dot(hs, wb2_ref[...], preferred_element_type=F32)

    first = (i % tiles_per_seq) == 0
    cat1_ref[0:SUBLANES, :] = jnp.where(first, s1_ref[...], carry_ref[0])
    cat2_ref[0:SUBLANES, :] = jnp.where(first, s2_ref[...], carry_ref[1])
    sub = tm // n_sub

    def project(r0):
        a = h_ref[r0:r0 + sub, :]
        cat1_ref[SUBLANES + r0:SUBLANES + r0 + sub, :] = jnp.dot(a, wb1_ref[...], preferred_element_type=F32)
        cat2_ref[SUBLANES + r0:SUBLANES + r0 + sub, :] = jnp.dot(a, wb2_ref[...], preferred_element_type=F32)

    def conv(r0, cw_ref, cb_ref, cat_ref):
        w = cw_ref[...]
        y = cat_ref[pl.ds(SUBLANES - 2 + r0, sub), :] * w[0:1, :] + cb_ref[...]
        y = y + cat_ref[pl.ds(SUBLANES - 1 + r0, sub), :] * w[1:2, :]
        return y + cat_ref[pl.ds(SUBLANES + r0, sub), :] * w[2:3, :]

    def gate(r0):
        y1 = conv(r0, cw1_ref, cb1_ref, cat1_ref)
        y2 = conv(r0, cw2_ref, cb2_ref, cat2_ref)
        g_ref[r0:r0 + sub, :] = (_silu(y1) * y2).astype(g_ref.dtype)

    project(0)
    for r in range(1, n_sub):
        project(r * sub)
        gate((r - 1) * sub)
    gate((n_sub - 1) * sub)
    tail1 = cat1_ref[tm:tm + SUBLANES, :]
    tail2 = cat2_ref[tm:tm + SUBLANES, :]
    carry_ref[0] = tail1
    carry_ref[1] = tail2
    t1_ref[...] = tail1
    t2_ref[...] = tail2


def ffn_up_fused(h, h_s, w_up, layer, state8, conv_w, conv_b, *, b, t, tm, tn, n_sub=1):
    m, k = h.shape
    ms = h_s.shape[0]
    c2 = w_up.shape[2]
    dff = c2 // 2
    nj = dff // tn
    assert t % tm == 0 and m == b * t and tm % SUBLANES == 0
    tps = t // tm
    wsp = lambda off: pl.BlockSpec((None, k, tn), lambda j, i: (layer, 0, j + off))
    ssp = lambda off: pl.BlockSpec((None, SUBLANES, tn), lambda j, i: (i // tps, 0, j + off))
    cwsp = lambda off: pl.BlockSpec((FFN_CONV, tn), lambda j, i: (0, j + off))
    cbsp = lambda off: pl.BlockSpec((1, tn), lambda j, i: (0, j + off))
    tail = pl.BlockSpec((None, SUBLANES, tn), lambda j, i: (i, 0, j))
    usp = pl.BlockSpec((ms, tn), lambda j, i: (0, j))
    cb = conv_b.reshape(1, c2)
    g, t1, t2, us1, us2 = pl.pallas_call(
        functools.partial(_ffn_up_fused_kernel, tiles_per_seq=tps, n_sub=n_sub),
        grid=(nj, m // tm),
        in_specs=[pl.BlockSpec((tm, k), lambda j, i: (i, 0)), pl.BlockSpec((ms, k), lambda j, i: (0, 0)),
                  wsp(0), wsp(nj), ssp(0), ssp(nj), cwsp(0), cwsp(nj), cbsp(0), cbsp(nj)],
        out_specs=[pl.BlockSpec((tm, tn), lambda j, i: (i, j)), tail, tail, usp, usp],
        out_shape=[jax.ShapeDtypeStruct((m, dff), BF16),
                   jax.ShapeDtypeStruct((m // tm, SUBLANES, dff), F32),
                   jax.ShapeDtypeStruct((m // tm, SUBLANES, dff), F32),
                   jax.ShapeDtypeStruct((ms, dff), F32),
                   jax.ShapeDtypeStruct((ms, dff), F32)],
        scratch_shapes=[pltpu.VMEM((k, tn), BF16),
                        pltpu.VMEM((k, tn), BF16),
                        pltpu.VMEM((SUBLANES + tm, tn), F32),
                        pltpu.VMEM((SUBLANES + tm, tn), F32),
                        pltpu.VMEM((2, SUBLANES, tn), F32)],
        compiler_params=_cparams(2),
        name="ffn_up_fused",
    )(h, h_s, w_up, w_up, state8, state8, conv_w, conv_w, cb, cb)
    tails = jnp.concatenate([t1[tps - 1::tps], t2[tps - 1::tps]], axis=-1)
    return g, tails, jnp.concatenate([us1, us2], axis=-1)


def _ssd_kernel(z_ref, x_ref, xh_ref, b_ref, bh_ref, c_ref, ch_ref, dt_ref,
                sx_ref, sb_ref, sc_ref, wx_ref, wb_ref, wc_ref, bx_ref, bb_ref, bc_ref,
                dtb_ref, alog_ref, dsk_ref, ng_ref, h0_ref,
                y_ref, hl_ref,
                st_ref, catx_ref, catb_ref, catc_ref, yacc_ref, *, t_valid, n_steps, cps):
    g = pl.program_id(1)
    c = pl.program_id(2)
    L = SSD_CHUNK
    rows_step = cps * L
    halo0 = SUBLANES - (SSM_CONV - 1)

    @pl.when(c == 0)
    def _():
        catx_ref[0:SUBLANES, :] = sx_ref[...]
        catb_ref[0:SUBLANES, :] = sb_ref[...]
        catc_ref[0:SUBLANES, :] = sc_ref[...]
        st_ref[...] = h0_ref[...].T

    @pl.when(c > 0)
    def _():
        catx_ref[0:SUBLANES, :] = xh_ref[...]
        catb_ref[0:SUBLANES, :] = bh_ref[...]
        catc_ref[0:SUBLANES, :] = ch_ref[...]

    catx_ref[SUBLANES:SUBLANES + rows_step, :] = x_ref[...]
    catb_ref[SUBLANES:SUBLANES + rows_step, :] = b_ref[...]
    catc_ref[SUBLANES:SUBLANES + rows_step, :] = c_ref[...]

    def conv_silu(cat_ref, w_ref, bias_ref, r0, sl):
        w = w_ref[:, sl]
        y = cat_ref[pl.ds(halo0 + r0, L), sl] * w[0:1, :] + bias_ref[:, sl]
        for k in range(1, SSM_CONV):
            y = y + cat_ref[pl.ds(halo0 + r0 + k, L), sl] * w[k:k + 1, :]
        return _silu(y)

    lanes0 = slice(0, LANES)
    row = lax.broadcasted_iota(jnp.int32, (L, LANES), 0)
    col = lax.broadcasted_iota(jnp.int32, (L, L), 1)
    causal = row >= col
    lane = lax.broadcasted_iota(jnp.int32, (1, LANES), 1)
    low = lane < SSM_HEAD_DIM
    a_neg = -jnp.exp(alog_ref[...])
    shift = (LANES - HEADS_PER_GROUP * g) % LANES

    for cc in range(cps):
        r0 = cc * L
        bm = conv_silu(catb_ref, wb_ref, bb_ref, r0, lanes0)
        cm = conv_silu(catc_ref, wc_ref, bc_ref, r0, lanes0)

        dtr = dt_ref[r0:r0 + L, :] + dtb_ref[...]
        dt = jnp.maximum(dtr, 0.0) + jnp.log(1.0 + jnp.exp(-jnp.abs(dtr)))
        dt = jnp.where(row + (c * rows_step + r0) < t_valid, dt, 0.0)
        ac = dt * a_neg
        s = 1
        while s < L:
            ac = ac + jnp.where(row >= s, pltpu.roll(ac, s, axis=0), 0.0)
            s *= 2
        ac_g = pltpu.roll(ac, shift, axis=1)
        dt_g = pltpu.roll(dt, shift, axis=1)
        ac_t = ac_g.T
        dt_t = dt_g.T

        bmb = bm.astype(BF16)
        cmb = cm.astype(BF16)
        cb = lax.dot_general(cmb, bmb, (((1,), (1,)), ((), ())), preferred_element_type=F32)
        bt = bm.T

        ssq = jnp.zeros((L, LANES), F32)
        for j in range(HEADS_PER_GROUP // 2):
            sl = slice(j * LANES, (j + 1) * LANES)
            xp = conv_silu(catx_ref, wx_ref, bx_ref, r0, sl)
            xpb = xp.astype(BF16)
            st_prev = st_ref[:, sl]
            yoff = jnp.dot(cmb, st_prev.astype(BF16), preferred_element_type=F32)
            parts = []
            for e in range(2):
                h = 2 * j + e
                ac_row = ac_t[h:h + 1, :]
                dt_row = dt_t[h:h + 1, :]
                ac_col = jnp.broadcast_to(ac_g[:, h:h + 1], (L, LANES))
                ac_last = ac_col[L - 1:L, :]
                decay = jnp.exp(jnp.where(causal, ac_col - ac_row, NEG_BIG))
                mh = (cb * decay * dt_row).astype(BF16)
                ydiag = jnp.dot(mh, xpb, preferred_element_type=F32)
                w_row = dt_row * jnp.exp(ac_last - ac_row)
                btw = (bt * w_row).astype(BF16)
                stc = jnp.dot(btw, xpb, preferred_element_type=F32)
                parts.append((ydiag, stc, jnp.exp(ac_col), jnp.exp(ac_last)))
            ydiag = jnp.where(low, parts[0][0], parts[1][0])
            stc = jnp.where(low, parts[0][1], parts[1][1])
            ecol = jnp.where(low, parts[0][2], parts[1][2])
            dec = jnp.where(low, parts[0][3], parts[1][3])
            y = ydiag + ecol * yoff + xp * dsk_ref[:, sl]
            y = y * _silu(z_ref[r0:r0 + L, sl])
            ssq = ssq + y * y
            yacc_ref[:, sl] = y
            st_ref[:, sl] = dec * st_prev + stc

        ms = jnp.sum(ssq, axis=-1, keepdims=True) * (1.0 / GROUP_WIDTH)
        rinv = lax.rsqrt(ms + NORM_EPS)
        y_ref[r0:r0 + L, :] = (yacc_ref[...] * rinv * ng_ref[...]).astype(y_ref.dtype)

    @pl.when(c == n_steps - 1)
    def _():
        hl_ref[...] = st_ref[...].T


def ssd_mixer(zx, dtraw, conv8, h0, conv_w, conv_b, dt_bias, a_log, d_skip, norm_g, *, t_valid):
    b, tp, _ = zx.shape
    nc = tp // SSD_CHUNK
    cps = next(d for d in (SSD_CHUNKS_PER_STEP, 2, 1) if nc % d == 0)
    ns = nc // cps
    L, gw, n = cps * SSD_CHUNK, GROUP_WIDTH, D_STATE
    hb = L // SUBLANES
    xoff = D_INNER // gw
    boff = (2 * D_INNER) // n
    coff = boff + N_SSM_GROUPS
    sboff = D_INNER // n
    scoff = sboff + N_SSM_GROUPS
    prev = lambda ci: jnp.maximum(ci * hb - 1, 0)
    in_specs = [
        pl.BlockSpec((None, L, gw), lambda bi, gi, ci: (bi, ci, gi)),
        pl.BlockSpec((None, L, gw), lambda bi, gi, ci: (bi, ci, xoff + gi)),
        pl.BlockSpec((None, SUBLANES, gw), lambda bi, gi, ci: (bi, prev(ci), xoff + gi)),
        pl.BlockSpec((None, L, n), lambda bi, gi, ci: (bi, ci, boff + gi)),
        pl.BlockSpec((None, SUBLANES, n), lambda bi, gi, ci: (bi, prev(ci), boff + gi)),
        pl.BlockSpec((None, L, n), lambda bi, gi, ci: (bi, ci, coff + gi)),
        pl.BlockSpec((None, SUBLANES, n), lambda bi, gi, ci: (bi, prev(ci), coff + gi)),
        pl.BlockSpec((None, L, N_SSM_HEADS), lambda bi, gi, ci: (bi, ci, 0)),
        pl.BlockSpec((None, SUBLANES, gw), lambda bi, gi, ci: (bi, 0, gi)),
        pl.BlockSpec((None, SUBLANES, n), lambda bi, gi, ci: (bi, 0, sboff + gi)),
        pl.BlockSpec((None, SUBLANES, n), lambda bi, gi, ci: (bi, 0, scoff + gi)),
        pl.BlockSpec((SSM_CONV, gw), lambda bi, gi, ci: (0, gi)),
        pl.BlockSpec((SSM_CONV, n), lambda bi, gi, ci: (0, sboff + gi)),
        pl.BlockSpec((SSM_CONV, n), lambda bi, gi, ci: (0, scoff + gi)),
        pl.BlockSpec((1, gw), lambda bi, gi, ci: (0, gi)),
        pl.BlockSpec((1, n), lambda bi, gi, ci: (0, sboff + gi)),
        pl.BlockSpec((1, n), lambda bi, gi, ci: (0, scoff + gi)),
        pl.BlockSpec((1, N_SSM_HEADS), lambda bi, gi, ci: (0, 0)),
        pl.BlockSpec((1, N_SSM_HEADS), lambda bi, gi, ci: (0, 0)),
        pl.BlockSpec((1, gw), lambda bi, gi, ci: (0, gi)),
        pl.BlockSpec((1, gw), lambda bi, gi, ci: (0, gi)),
        pl.BlockSpec((None, None, gw, n), lambda bi, gi, ci: (bi, gi, 0, 0)),
    ]
    d_rep = jnp.repeat(d_skip, SSM_HEAD_DIM).reshape(1, D_INNER)
    cb2 = conv_b.reshape(1, CONV_DIM)
    y, hl = pl.pallas_call(
        functools.partial(_ssd_kernel, t_valid=t_valid, n_steps=ns, cps=cps),
        grid=(b, N_SSM_GROUPS, ns),
        in_specs=in_specs,
        out_specs=[pl.BlockSpec((None, L, gw), lambda bi, gi, ci: (bi, ci, gi)),
                   pl.BlockSpec((None, None, gw, n), lambda bi, gi, ci: (bi, gi, 0, 0))],
        out_shape=[jax.ShapeDtypeStruct((b, tp, D_INNER), BF16),
                   jax.ShapeDtypeStruct((b, N_SSM_GROUPS, gw, n), F32)],
        scratch_shapes=[pltpu.VMEM((n, gw), F32),
                        pltpu.VMEM((SUBLANES + L, gw), F32),
                        pltpu.VMEM((SUBLANES + L, n), F32),
                        pltpu.VMEM((SUBLANES + L, n), F32),
                        pltpu.VMEM((SSD_CHUNK, gw), F32)],
        compiler_params=_cparams(3),
        name="ssd_mixer",
    )(zx, zx, zx, zx, zx, zx, zx, dtraw, conv8, conv8, conv8, conv_w, conv_w, conv_w,
      cb2, cb2, cb2, dt_bias.reshape(1, -1), a_log.reshape(1, -1), d_rep,
      norm_g.reshape(1, D_INNER), h0.reshape(b, N_SSM_GROUPS, gw, n))
    return y, hl.reshape(b, N_SSM_HEADS, SSM_HEAD_DIM, n)


def _rel_bucket(dist):
    max_exact = N_BUCKETS // 2
    nn = np.asarray(dist, dtype=np.int64)
    ratio = np.log(np.maximum(nn, 1) / max_exact) / np.log(REL_MAX_DIST / max_exact)
    large = np.minimum(max_exact + (np.maximum(ratio, 0.0) * (N_BUCKETS - max_exact)).astype(np.int64),
                       N_BUCKETS - 1)
    return np.where(nn < max_exact, nn, large).astype(np.int32)


def _bias_by_step(rel_bias, g):
    dists = np.arange(Q_BLOCK + 1) * DIL_RATES[g]
    return rel_bias[_rel_bucket(dists)][:, g * N_SLOTS:(g + 1) * N_SLOTS].astype(F32)


def _toeplitz_bias(rel_bias):
    ql = lax.broadcasted_iota(jnp.int32, (Q_BLOCK, 2 * Q_BLOCK), 0)
    kl = lax.broadcasted_iota(jnp.int32, (Q_BLOCK, 2 * Q_BLOCK), 1)
    step = ql + Q_BLOCK - kl
    ok = (step >= 0) & (step <= Q_BLOCK)
    onehot = (step[:, :, None] == jnp.arange(Q_BLOCK + 1)[None, None, :]).astype(F32)
    tabs = jnp.stack([_bias_by_step(rel_bias, g) for g in range(N_DIL_GROUPS)])
    tiles = jnp.einsum("qkx,gxs->gsqk", onehot, tabs, precision=lax.Precision.HIGHEST)
    return jnp.where(ok[None, None], tiles, NEG_BIG)


def _attn_prompt_kernel(q0_ref, q1_ref, q2_ref, k_ref, v_ref, bias_ref, o_ref,
                        qf_ref, acc_ref, m_ref, l_ref, *, t):
    q_refs = (q0_ref, q1_ref, q2_ref)
    col = lax.broadcasted_iota(jnp.int32, (KV_REP * Q_BLOCK, 2 * Q_BLOCK), 1)
    for g in range(N_DIL_GROUPS):
        rate = DIL_RATES[g]
        span = rate * Q_BLOCK
        nqb = t // span
        for r in range(KV_REP):
            qf_ref[r] = q_refs[g][:, r * HEAD_DIM:(r + 1) * HEAD_DIM].astype(F32)

        def rows_at(start, rate=rate):
            return pl.ds(start, Q_BLOCK) if rate == 1 else pl.ds(start, Q_BLOCK, stride=rate)

        def body(idx, carry, g=g, span=span, nqb=nqb, rows_at=rows_at):
            rho = idx // nqb
            qb = idx - rho * nqb
            start = rho + span * qb
            start_prev = jnp.maximum(start - span, rho)
            if span == Q_BLOCK:
                start = pl.multiple_of(start, Q_BLOCK)
                start_prev = pl.multiple_of(start_prev, Q_BLOCK)
            rows = rows_at(start)
            rows_prev = rows_at(start_prev)
            qs = jnp.concatenate([qf_ref[r, rows, :] for r in range(KV_REP)], axis=0).astype(BF16)
            k2 = jnp.concatenate([k_ref[rows_prev, :], k_ref[rows, :]], axis=0).astype(BF16)
            v2 = jnp.concatenate([v_ref[rows_prev, :], v_ref[rows, :]], axis=0).astype(BF16)
            s = lax.dot_general(qs, k2, (((1,), (1,)), ((), ())), preferred_element_type=F32)
            s = s + bias_ref[g].reshape(KV_REP * Q_BLOCK, 2 * Q_BLOCK)
            s = jnp.where((col >= Q_BLOCK) | (qb > 0), s, NEG_BIG)
            mg = jnp.max(s, axis=-1, keepdims=True)
            p = jnp.exp(s - mg)
            lg = jnp.sum(p, axis=-1, keepdims=True)
            og = jnp.dot(p.astype(BF16), v2, preferred_element_type=F32)
            for r in range(KV_REP):
                rs = slice(r * Q_BLOCK, (r + 1) * Q_BLOCK)
                o_r = og[rs]
                m_r = jnp.broadcast_to(mg[rs], (Q_BLOCK, HEAD_DIM))
                l_r = jnp.broadcast_to(lg[rs], (Q_BLOCK, HEAD_DIM))
                if g > 0:
                    m_old = m_ref[r, rows, :]
                    m_new = jnp.maximum(m_old, m_r)
                    a_old = jnp.exp(m_old - m_new)
                    a_new = jnp.exp(m_r - m_new)
                    o_r = acc_ref[r, rows, :] * a_old + o_r * a_new
                    l_r = l_ref[r, rows, :] * a_old + l_r * a_new
                    m_r = m_new
                if g == N_DIL_GROUPS - 1:
                    acc_ref[r, rows, :] = o_r / l_r
                else:
                    acc_ref[r, rows, :] = o_r
                    m_ref[r, rows, :] = m_r
                    l_ref[r, rows, :] = l_r
            return carry

        lax.fori_loop(0, rate * nqb, body, 0, unroll=4)
    for r in range(KV_REP):
        o_ref[:, r * HEAD_DIM:(r + 1) * HEAD_DIM] = acc_ref[r].astype(o_ref.dtype)


def attention_prompt(q, k, v, bias_t, b, t):
    rw = KV_REP * HEAD_DIM
    qspec = lambda g: pl.BlockSpec((t, rw), lambda bi, hi: (bi, g * N_KV_HEADS + hi))
    kvspec = pl.BlockSpec((None, t, HEAD_DIM), lambda bi, hi: (bi, 0, hi))
    return pl.pallas_call(
        functools.partial(_attn_prompt_kernel, t=t),
        grid=(b, N_KV_HEADS),
        in_specs=[qspec(0), qspec(1), qspec(2), kvspec, kvspec,
                  pl.BlockSpec((N_DIL_GROUPS, KV_REP, Q_BLOCK, 2 * Q_BLOCK), lambda bi, hi: (0, hi, 0, 0))],
        out_specs=pl.BlockSpec((t, rw), lambda bi, hi: (bi, hi)),
        out_shape=jax.ShapeDtypeStruct((b * t, N_SLOTS * HEAD_DIM), BF16),
        scratch_shapes=[pltpu.VMEM((KV_REP, t, HEAD_DIM), F32)] * 4,
        compiler_params=_cparams(2),
        name="attn_prompt",
    )(q, q, q, k, v, bias_t)


def _attn_sample_kernel(q_ref, kc_ref, kn_ref, vc_ref, vn_ref, bm_ref, bs_ref, o_ref,
                        kbuf_ref, vbuf_ref, *, t_new):
    w = WINDOW_MAX
    rows = N_DIL_GROUPS * t_new * KV_REP
    kbuf_ref[0:w, :] = kc_ref[...]
    kbuf_ref[w:w + SUBLANES, :] = kn_ref[...]
    vbuf_ref[0:w, :] = vc_ref[...]
    vbuf_ref[w:w + SUBLANES, :] = vn_ref[...]
    q = q_ref[...]
    qf = q.astype(F32)
    ridx = lax.broadcasted_iota(jnp.int32, (rows, 1), 0)

    def row_mask(g, j):
        base = (g * t_new + j) * KV_REP
        return (ridx >= base) & (ridx < base + KV_REP)

    kself = jnp.zeros((rows, HEAD_DIM), F32)
    vself = jnp.zeros((rows, HEAD_DIM), F32)
    for j in range(t_new):
        tok_j = row_mask(0, j)
        for g in range(1, N_DIL_GROUPS):
            tok_j = tok_j | row_mask(g, j)
        kself = jnp.where(tok_j, kn_ref[j:j + 1, :].astype(BF16).astype(F32), kself)
        vself = jnp.where(tok_j, vn_ref[j:j + 1, :], vself)
    s_self = jnp.sum(qf * kself, axis=-1, keepdims=True) + bs_ref[:, 0:1]
    s = jnp.zeros((rows, Q_BLOCK), F32)
    for g in range(N_DIL_GROUPS):
        for j in range(t_new):
            start = w + j - Q_BLOCK * DIL_RATES[g]
            ks = kbuf_ref[pl.ds(start, Q_BLOCK, stride=DIL_RATES[g]), :].astype(BF16)
            sg = lax.dot_general(q, ks, (((1,), (1,)), ((), ())), preferred_element_type=F32)
            s = jnp.where(row_mask(g, j), sg, s)
    s = s + bm_ref[...]
    m = jnp.maximum(jnp.max(s, axis=-1, keepdims=True), s_self)
    p = jnp.exp(s - m)
    p_self = jnp.exp(s_self - m)
    l = jnp.sum(p, axis=-1, keepdims=True) + p_self
    pb = p.astype(BF16)
    acc = p_self * vself
    for g in range(N_DIL_GROUPS):
        for j in range(t_new):
            start = w + j - Q_BLOCK * DIL_RATES[g]
            vs = vbuf_ref[pl.ds(start, Q_BLOCK, stride=DIL_RATES[g]), :].astype(BF16)
            og = jnp.dot(pb, vs, preferred_element_type=F32)
            acc = acc + jnp.where(row_mask(g, j), og, 0.0)
    o = acc / l
    lse = m + jnp.log(l)
    n = t_new * KV_REP
    ls = [lse[g * n:(g + 1) * n] for g in range(N_DIL_GROUPS)]
    mm = jnp.maximum(jnp.maximum(ls[0], ls[1]), ls[2])
    es = [jnp.exp(x - mm) for x in ls]
    inv = 1.0 / (es[0] + es[1] + es[2])
    out = sum((es[g] * inv) * o[g * n:(g + 1) * n] for g in range(N_DIL_GROUPS))
    o_ref[...] = out.astype(o_ref.dtype)


def attention_sample(q, k_new, v_new, k_cache, v_cache, rel_bias, b, t_new):
    assert PAST_LEN >= WINDOW_MAX and t_new <= SUBLANES
    rows = N_DIL_GROUPS * t_new * KV_REP
    n = t_new * KV_REP
    qs = q.reshape(b, t_new, N_DIL_GROUPS, N_KV_HEADS, KV_REP, HEAD_DIM)
    qs = jnp.transpose(qs, (0, 3, 2, 1, 4, 5)).reshape(b, N_KV_HEADS, rows, HEAD_DIM)
    pad = ((0, 0), (0, SUBLANES - t_new), (0, 0))
    kn = jnp.pad(k_new, pad)
    vn = jnp.pad(v_new, pad)
    bm, bs = [], []
    for g in range(N_DIL_GROUPS):
        tab = _bias_by_step(rel_bias, g)
        tab = tab.reshape(Q_BLOCK + 1, N_KV_HEADS, KV_REP)
        main = jnp.transpose(tab[1:][::-1], (1, 2, 0))
        bm.append(jnp.broadcast_to(main[:, None], (N_KV_HEADS, t_new, KV_REP, Q_BLOCK)))
        bs.append(jnp.broadcast_to(tab[0][:, None, :, None], (N_KV_HEADS, t_new, KV_REP, LANES)))
    bm = jnp.stack(bm, axis=1).reshape(N_KV_HEADS, rows, Q_BLOCK)
    bs = jnp.stack(bs, axis=1).reshape(N_KV_HEADS, rows, LANES)
    cache = pl.BlockSpec((None, WINDOW_MAX, HEAD_DIM), lambda bi, hi: (bi, 0, hi))
    new = pl.BlockSpec((None, SUBLANES, HEAD_DIM), lambda bi, hi: (bi, 0, hi))
    o = pl.pallas_call(
        functools.partial(_attn_sample_kernel, t_new=t_new),
        grid=(b, N_KV_HEADS),
        in_specs=[pl.BlockSpec((None, None, rows, HEAD_DIM), lambda bi, hi: (bi, hi, 0, 0)),
                  cache, new, cache, new,
                  pl.BlockSpec((None, rows, Q_BLOCK), lambda bi, hi: (hi, 0, 0)),
                  pl.BlockSpec((None, rows, LANES), lambda bi, hi: (hi, 0, 0))],
        out_specs=pl.BlockSpec((None, None, n, HEAD_DIM), lambda bi, hi: (bi, hi, 0, 0)),
        out_shape=jax.ShapeDtypeStruct((b, N_KV_HEADS, n, HEAD_DIM), BF16),
        scratch_shapes=[pltpu.VMEM((WINDOW_MAX + SUBLANES, HEAD_DIM), F32),
                        pltpu.VMEM((WINDOW_MAX + SUBLANES, HEAD_DIM), F32)],
        compiler_params=_cparams(2),
        name="attn_sample",
    )(qs, k_cache, kn, v_cache, vn, bm, bs)
    o = o.reshape(b, N_KV_HEADS, t_new, KV_REP, HEAD_DIM)
    return jnp.transpose(o, (0, 2, 1, 3, 4)).reshape(b * t_new, N_SLOTS * HEAD_DIM)


def _pad_rows_front(state, rows):
    return jnp.pad(state, ((0, 0), (rows - state.shape[1], 0), (0, 0)))


TM_PROMPT = 1024
TN_WS = 512
TN_FFN_UP = 256
TM_LONGK = 512
TN_SSM_OUT = 512
TN_FFN_DOWN = 512
TM_NORM = 512
TC_GATE_SAMPLE = D_FF // 2


def _ssd_block(zx, dtraw, b, t, conv0, h0, layer, W):
    zx = zx.reshape(b, t, XBC_END)
    dtraw = dtraw.reshape(b, t, N_SSM_HEADS)
    conv_new = zx[:, t - (SSM_CONV - 1):, D_INNER:]
    tp = -(-t // SSD_CHUNK) * SSD_CHUNK
    if tp != t:
        zx = jnp.pad(zx, ((0, 0), (0, tp - t), (0, 0)))
        dtraw = jnp.pad(dtraw, ((0, 0), (0, tp - t), (0, 0)))
    y, hs = ssd_mixer(zx, dtraw, _pad_rows_front(conv0, SUBLANES), h0,
                      W["ssm_conv_w"][layer], W["ssm_conv_b"][layer], W["ssm_dt_bias"][layer],
                      W["ssm_a_log"][layer], W["ssm_d"][layer], W["ssm_norm"][layer], t_valid=t)
    return y[:, :t].reshape(b * t, D_INNER), hs, conv_new


def _trunk(xp, xs, pp, ps, bp, tp, bs, ts, ssm0_s, convs0_s, convf0_s, k_cache, v_cache, W):
    ms = bs * ts
    ssm_p, ssm_s, convs_p, convs_s, convf_p, convf_s = [], [], [], [], [], []
    kp = vp = ks = vs = None
    zero_ssm = jnp.zeros((bp,) + ssm0_s.shape[2:], ssm0_s.dtype)
    zero_convs = jnp.zeros((bp,) + convs0_s.shape[2:], convs0_s.dtype)
    zero_convf8 = jnp.zeros((bp, SUBLANES, 2 * D_FF), convf0_s.dtype)
    norm2 = lambda gain: (rmsnorm(xp, gain, TM_NORM), rmsnorm(xs, gain, ms))
    ws = functools.partial(ws_matmul, tm=TM_PROMPT)
    for i in range(DEPTH):
        if i < N_A_LAYERS:
            hp, hs = norm2(W["norm_mix"][i])
            zx_p, zx_s = ws(hp, hs, W["ssm_w_in"], i, n_cols=XBC_END, tn=TN_WS, name="ssm_in")
            dt_p, dt_s = ws(hp, hs, W["ssm_w_in"], i, n_cols=N_SSM_HEADS, col0=XBC_END, tn=N_SSM_HEADS,
                            name="ssm_in_dt")
            y_p, h_p, c_p = _ssd_block(zx_p, dt_p, bp, tp, zero_convs, zero_ssm, i, W)
            y_s, h_s, c_s = _ssd_block(zx_s, dt_s, bs, ts, convs0_s[i], ssm0_s[i], i, W)
            ssm_p.append(h_p); ssm_s.append(h_s); convs_p.append(c_p); convs_s.append(c_s)
            xp = mm_residual(y_p, W["ssm_w_out"], i, xp, tm=TM_LONGK, tn=TN_SSM_OUT, name="ssm_out")
            xs = mm_residual(y_s, W["ssm_w_out"], i, xs, tm=ms, tn=TN_SSM_OUT, name="ssm_out")
        else:
            li = i - N_A_LAYERS
            if li == 0:
                hp, hs = norm2(W["norm_kv"])
                kp, ks = ws(hp, hs, W["w_k"], 0, n_cols=N_KV_HEADS * HEAD_DIM, tn=TN_WS, kind="headnorm",
                            extras_p=(W["k_gain"],), name="k_proj")
                vp, vs = ws(hp, hs, W["w_v"], 0, n_cols=N_KV_HEADS * HEAD_DIM, tn=TN_WS, name="v_proj")
                kp, vp = (a.reshape(bp, tp, N_KV_HEADS * HEAD_DIM) for a in (kp, vp))
                ks, vs = (a.reshape(bs, ts, N_KV_HEADS * HEAD_DIM) for a in (ks, vs))
            hp, hs = norm2(W["norm_mix"][i])
            qp, qs = ws(hp, hs, W["attn_w_q"], li, n_cols=N_DIL_GROUPS * D_MODEL, tn=TN_WS, kind="headnorm",
                        extras_p=(W["q_gain"][li],), out_dtype=BF16, name="q_proj")
            op = attention_prompt(qp, kp, vp, W["bias_tiles"], bp, tp)
            os_ = attention_sample(qs, ks, vs, k_cache, v_cache, W["rel_bias"], bs, ts)
            xp, xs = ws(op, os_, W["attn_w_o"], li, n_cols=D_MODEL, tn=TN_WS, kind="residual",
                        extras_p=(xp,), extras_s=(xs,), name="attn_out")
        hp, hs = norm2(W["norm_ffn"][i])
        g_p, tails, u_s = ffn_up_fused(hp, hs, W["ffn_w_up"], i, zero_convf8, W["ffn_conv_w"][i],
                                       W["ffn_conv_b"][i], b=bp, t=tp, tm=TM_PROMPT, tn=TN_FFN_UP)
        convf_p.append(tails[:, SUBLANES - (FFN_CONV - 1):])
        u_s = u_s.reshape(bs, ts, 2 * D_FF)
        convf_s.append(u_s[:, ts - (FFN_CONV - 1):])
        tpad = -(-ts // SUBLANES) * SUBLANES
        u_s = jnp.pad(u_s, ((0, 0), (0, tpad - ts), (0, 0)))
        g_s = ffn_gate(u_s, _pad_rows_front(convf0_s[i], SUBLANES), W["ffn_conv_w"][i], W["ffn_conv_b"][i],
                       tt=tpad, tc=TC_GATE_SAMPLE)[:, :ts].reshape(ms, D_FF)
        xp = mm_residual(g_p, W["ffn_w_down"], i, xp, tm=TM_LONGK, tn=TN_FFN_DOWN, name="ffn_down")
        xs = mm_residual(g_s, W["ffn_w_down"], i, xs, tm=ms, tn=TN_FFN_DOWN, name="ffn_down")
        hp, hs = norm2(W["norm_ple"][i])
        xp, xs = ws(hp, hs, W["ple_w_gate"], i, n_cols=D_MODEL, tn=TN_WS, kind="ple",
                    extras_p=(pp[i], xp), extras_s=(ps[i], xs), aux=W["ple_w_proj"], name="ple_gate")
    st = jnp.stack
    return (xp, xs, st(ssm_p), st(ssm_s), st(convs_p), st(convs_s), st(convf_p), st(convf_s), kp, vp, ks, vs)


def kernel(x_prompt, x_sample, p_prompt, p_sample, state_ssm, state_conv_ssm, state_conv_ffn, cache_k_window, cache_v_window, norm_mix, norm_ffn, norm_ple, ssm_w_in, ssm_conv_w, ssm_conv_b, ssm_dt_bias, ssm_a_log, ssm_d, ssm_norm, ssm_w_out, norm_kv, w_k, w_v, k_norm, attn_w_q, q_norm, attn_w_o, rel_bias, ffn_w_up, ffn_conv_w, ffn_conv_b, ffn_w_down, ple_w_proj, ple_w_gate):
    bp, tp, _ = x_prompt.shape
    bs, ts, _ = x_sample.shape
    assert tp % (Q_BLOCK * max(DIL_RATES)) == 0 and tp <= WINDOW_MAX
    assert cache_k_window.shape[1] == WINDOW_MAX
    q_gain = jnp.broadcast_to(q_norm[:, :, None, :], (DEPTH - N_A_LAYERS, N_DIL_GROUPS, N_SLOTS, HEAD_DIM))
    W = dict(
        norm_mix=norm_mix, norm_ffn=norm_ffn, norm_ple=norm_ple, norm_kv=norm_kv,
        ssm_w_in=ssm_w_in, ssm_conv_w=ssm_conv_w, ssm_conv_b=ssm_conv_b, ssm_dt_bias=ssm_dt_bias,
        ssm_a_log=ssm_a_log, ssm_d=ssm_d, ssm_norm=ssm_norm, ssm_w_out=ssm_w_out.astype(BF16),
        w_k=w_k[None], w_v=w_v[None], k_gain=jnp.tile(k_norm, N_KV_HEADS).reshape(1, -1),
        attn_w_q=attn_w_q, q_gain=(q_gain * (HEAD_DIM ** -0.5)).reshape(DEPTH - N_A_LAYERS, 1, -1),
        attn_w_o=attn_w_o, rel_bias=rel_bias, bias_tiles=_toeplitz_bias(rel_bias),
        ffn_w_up=ffn_w_up, ffn_conv_w=ffn_conv_w, ffn_conv_b=ffn_conv_b,
        ffn_w_down=ffn_w_down.astype(BF16), ple_w_proj=ple_w_proj, ple_w_gate=ple_w_gate)

    kc = cache_k_window.reshape(bs, WINDOW_MAX, N_KV_HEADS * HEAD_DIM)
    vc = cache_v_window.reshape(bs, WINDOW_MAX, N_KV_HEADS * HEAD_DIM)
    (y_p, y_s, ssm_p, ssm_s, convs_p, convs_s, convf_p, convf_s, k_p, v_p, k_s, v_s) = _trunk(
        x_prompt.reshape(bp * tp, D_MODEL), x_sample.reshape(bs * ts, D_MODEL),
        p_prompt.reshape(DEPTH, bp * tp, PLE_DIM).astype(BF16),
        p_sample.reshape(DEPTH, bs * ts, PLE_DIM).astype(BF16),
        bp, tp, bs, ts, state_ssm, state_conv_ssm, state_conv_ffn, kc, vc, W)
    kv4 = lambda a, b_, t_: a.reshape(b_, t_, N_KV_HEADS, HEAD_DIM)
    return (y_p.reshape(bp, tp, D_MODEL), y_s.reshape(bs, ts, D_MODEL), ssm_p, ssm_s, convs_p, convs_s,
            convf_p, convf_s, kv4(k_p, bp, tp), kv4(v_p, bp, tp), kv4(k_s, bs, ts), kv4(v_s, bs, ts))
```

```python
import functools

import numpy as np
import jax
import jax.numpy as jnp
from jax import lax
from jax.experimental import pallas as pl
from jax.experimental.pallas import tpu as pltpu

F32 = jnp.float32
BF16 = jnp.bfloat16

D_MODEL = 4096
DEPTH = 4
N_A_LAYERS = DEPTH // 2
PAST_LEN = 8192
D_INNER = 2 * D_MODEL
SSM_HEAD_DIM = 64
N_SSM_HEADS = D_INNER // SSM_HEAD_DIM
D_STATE = 128
N_SSM_GROUPS = 8
HEADS_PER_GROUP = N_SSM_HEADS // N_SSM_GROUPS
GROUP_WIDTH = D_INNER // N_SSM_GROUPS
SSM_CONV = 4
SSD_CHUNK = 128
SSD_CHUNKS_PER_STEP = 8
CONV_DIM =D_INNER + 2 * N_SSM_GROUPS * D_STATE
XBC_END = D_INNER + CONV_DIM
HEAD_DIM = 128
N_SLOTS = D_MODEL // HEAD_DIM
N_KV_HEADS = 8
KV_REP = N_SLOTS // N_KV_HEADS
DIL_WINDOWS = (128, 512, 2048)
DIL_RATES = (1, 4, 16)
N_DIL_GROUPS = len(DIL_WINDOWS)
WINDOW_MAX = max(DIL_WINDOWS)
Q_BLOCK = 128
N_BUCKETS = 32
REL_MAX_DIST = WINDOW_MAX
D_FF = 256 * ((8 * D_MODEL // 3 + 255) // 256)
FFN_CONV = 3
PLE_DIM = 256
NORM_EPS = 1e-6

LANES = 128
SUBLANES = 8
VMEM_LIMIT_BYTES = 56 * 1024 * 1024
NEG_BIG = -1e30


def _cparams(n_axes):
    return pltpu.CompilerParams(dimension_semantics=("arbitrary",) * n_axes,
                                vmem_limit_bytes=VMEM_LIMIT_BYTES)


def _sigmoid(x):
    return 0.5 * jnp.tanh(0.5 * x) + 0.5


def _silu(x):
    h = 0.5 * x
    return h * jnp.tanh(h) + h


def _rmsnorm_kernel(x_ref, g_ref, o_ref):
    x = x_ref[...]
    ms = jnp.mean(x * x, axis=-1, keepdims=True)
    o_ref[...] = (x * lax.rsqrt(ms + NORM_EPS) * g_ref[...]).astype(o_ref.dtype)


def rmsnorm(x, gain, tm):
    m, d = x.shape
    return pl.pallas_call(
        _rmsnorm_kernel,
        grid=(m // tm,),
        in_specs=[pl.BlockSpec((tm, d), lambda i: (i, 0)),
                  pl.BlockSpec((1, d), lambda i: (0, 0))],
        out_specs=pl.BlockSpec((tm, d), lambda i: (i, 0)),
        out_shape=jax.ShapeDtypeStruct((m, d), BF16),
        compiler_params=_cparams(1),
        name="rmsnorm",
    )(x, gain.reshape(1, d))


def _headnorm(acc, gain):
    cols = []
    for c in range(acc.shape[1] // HEAD_DIM):
        blk = acc[:, c * HEAD_DIM:(c + 1) * HEAD_DIM]
        ms = jnp.mean(blk * blk, axis=-1, keepdims=True)
        cols.append(blk * lax.rsqrt(ms + NORM_EPS) * gain[:, c * HEAD_DIM:(c + 1) * HEAD_DIM])
    return jnp.concatenate(cols, axis=1)


def _mm_res_kernel(a_ref, w_ref, res_ref, o_ref):
    o_ref[...] = res_ref[...] + jnp.dot(a_ref[...], w_ref[...], preferred_element_type=F32)


def mm_residual(a, w, layer, res, *, tm, tn, name):
    m, k = a.shape
    n = w.shape[2]
    assert m % tm == 0 and n % tn == 0, (m, tm, n, tn)
    return pl.pallas_call(
        _mm_res_kernel,
        grid=(m // tm, n // tn),
        in_specs=[pl.BlockSpec((tm, k), lambda i, j: (i, 0)),
                  pl.BlockSpec((None, k, tn), lambda i, j: (layer, 0, j)),
                  pl.BlockSpec((tm, tn), lambda i, j: (i, j))],
        out_specs=pl.BlockSpec((tm, tn), lambda i, j: (i, j)),
        out_shape=jax.ShapeDtypeStruct((m, n), F32),
        compiler_params=_cparams(2),
        name=name,
    )(a, w, res)


def _ws_epilogue(kind, acc, ins, aux, o_ref):
    if kind == "plain":
        o_ref[...] = acc.astype(o_ref.dtype)
    elif kind == "residual":
        o_ref[...] = ins[0][...] + acc
    elif kind == "headnorm":
        o_ref[...] = _headnorm(acc, ins[0][...]).astype(o_ref.dtype)
    else:
        proj = jnp.dot(ins[0][...], aux[...], preferred_element_type=F32)
        o_ref[...] = ins[1][...] + _sigmoid(acc) * proj


def _ws_kernel(*refs, kind):
    ap_ref, as_ref, w_ref = refs[:3]
    n_extra = {"plain": 0, "residual": 1, "headnorm": 1, "ple": 2}[kind]
    pos = 3
    aux_ref = None
    if kind == "ple":
        aux_ref = refs[pos]
        pos += 1
    if kind == "headnorm":
        ins_p = ins_s = refs[pos:pos + 1]
        pos += 1
    else:
        ins_p = refs[pos:pos + n_extra]
        ins_s = refs[pos + n_extra:pos + 2 * n_extra]
        pos += 2 * n_extra
    op_ref, os_ref, wb_ref = refs[pos:pos + 3]
    auxb_ref = refs[pos + 3] if kind == "ple" else None

    @pl.when(pl.program_id(1) == 0)
    def _():
        wb_ref[...] = w_ref[...].astype(BF16)
        if kind == "ple":
            auxb_ref[...] = aux_ref[...].astype(BF16)
        acc_s = jnp.dot(as_ref[...], wb_ref[...], preferred_element_type=F32)
        _ws_epilogue(kind, acc_s, ins_s, auxb_ref, os_ref)

    acc = jnp.dot(ap_ref[...], wb_ref[...], preferred_element_type=F32)
    _ws_epilogue(kind, acc, ins_p, auxb_ref, op_ref)


def ws_matmul(a_p, a_s, w, layer, *, n_cols, col0=0, tm, tn, kind="plain", extras_p=(), extras_s=(),
              aux=None, out_dtype=F32, name):
    mp, k = a_p.shape
    ms = a_s.shape[0]
    assert mp % tm == 0 and n_cols % tn == 0 and col0 % tn == 0
    nj, ni, c0 = n_cols // tn, mp // tm, col0 // tn
    in_specs = [pl.BlockSpec((tm, k), lambda j, i: (i, 0)),
                pl.BlockSpec((ms, k), lambda j, i: (0, 0)),
                pl.BlockSpec((None, k, tn), lambda j, i: (layer, 0, c0 + j))]
    args = [a_p, a_s, w]
    scratch = [pltpu.VMEM((k, tn), BF16)]
    tile_p = pl.BlockSpec((tm, tn), lambda j, i: (i, j))
    tile_s = pl.BlockSpec((ms, tn), lambda j, i: (0, j))
    if kind == "ple":
        pd = aux.shape[1]
        in_specs.append(pl.BlockSpec((None, pd, tn), lambda j, i: (layer, 0, c0 + j)))
        args.append(aux)
        scratch.append(pltpu.VMEM((pd, tn), BF16))
        in_specs += [pl.BlockSpec((tm, pd), lambda j, i: (i, 0)), tile_p,
                     pl.BlockSpec((ms, pd), lambda j, i: (0, 0)), tile_s]
    elif kind == "residual":
        in_specs += [tile_p, tile_s]
    elif kind == "headnorm":
        in_specs.append(pl.BlockSpec((1, tn), lambda j, i: (0, j)))
    args += list(extras_p) + list(extras_s)
    return pl.pallas_call(
        functools.partial(_ws_kernel, kind=kind),
        grid=(nj, ni),
        in_specs=in_specs,
        out_specs=[tile_p, tile_s],
        out_shape=[jax.ShapeDtypeStruct((mp, n_cols), out_dtype),
                   jax.ShapeDtypeStruct((ms, n_cols), out_dtype)],
        scratch_shapes=scratch,
        compiler_params=_cparams(2),
        name=name,
    )(*args)


def _ffn_gate_kernel(u1_ref, u2_ref, h1_ref, h2_ref, s1_ref, s2_ref, w1_ref, w2_ref,
                     b1_ref, b2_ref, o_ref, cat1_ref, cat2_ref):
    tb = pl.program_id(1)
    tt = u1_ref.shape[0]

    def conv(u_ref, halo_ref, state_ref, w_ref, b_ref, cat_ref):
        @pl.when(tb == 0)
        def _():
            cat_ref[0:SUBLANES, :] = state_ref[...]

        @pl.when(tb > 0)
        def _():
            cat_ref[0:SUBLANES, :] = halo_ref[...]

        cat_ref[SUBLANES:SUBLANES + tt, :] = u_ref[...]
        w = w_ref[...]
        y = cat_ref[pl.ds(SUBLANES - 2, tt), :] * w[0:1, :] + b_ref[...]
        y = y + cat_ref[pl.ds(SUBLANES - 1, tt), :] * w[1:2, :]
        y = y + cat_ref[pl.ds(SUBLANES, tt), :] * w[2:3, :]
        return y

    y1 = conv(u1_ref, h1_ref, s1_ref, w1_ref, b1_ref, cat1_ref)
    y2 = conv(u2_ref, h2_ref, s2_ref, w2_ref, b2_ref, cat2_ref)
    o_ref[...] = (_silu(y1) * y2).astype(o_ref.dtype)


def ffn_gate(u, state8, conv_w, conv_b, *, tt, tc):
    b, t, c2 = u.shape
    dff = c2 // 2
    nj = dff // tc
    hb = tt // SUBLANES
    cur = lambda off: pl.BlockSpec((None, tt, tc), lambda bi, ti, j: (bi, ti, j + off))
    halo = lambda off: pl.BlockSpec(
        (None, SUBLANES, tc), lambda bi, ti, j: (bi, jnp.maximum(ti * hb - 1, 0), j + off))
    st = lambda off: pl.BlockSpec((None, SUBLANES, tc), lambda bi, ti, j: (bi, 0, j + off))
    wsp = lambda off: pl.BlockSpec((FFN_CONV, tc), lambda bi, ti, j: (0, j + off))
    bsp = lambda off: pl.BlockSpec((1, tc), lambda bi, ti, j: (0, j + off))
    return pl.pallas_call(
        _ffn_gate_kernel,
        grid=(b, t // tt, nj),
        in_specs=[cur(0), cur(nj), halo(0), halo(nj), st(0), st(nj), wsp(0), wsp(nj),
                  bsp(0), bsp(nj)],
        out_specs=pl.BlockSpec((None, tt, tc), lambda bi, ti, j: (bi, ti, j)),
        out_shape=jax.ShapeDtypeStruct((b, t, dff), BF16),
        scratch_shapes=[pltpu.VMEM((SUBLANES + tt, tc), F32),
                        pltpu.VMEM((SUBLANES + tt, tc), F32)],
        compiler_params=_cparams(3),
        name="ffn_gate",
    )(u, u, u, u, state8, state8, conv_w, conv_w, conv_b.reshape(1, c2), conv_b.reshape(1, c2))


def _ffn_up_fused_kernel(h_ref, hs_ref, w1_ref, w2_ref, s1_ref, s2_ref, cw1_ref, cw2_ref, cb1_ref, cb2_ref,
                         g_ref, t1_ref, t2_ref, us1_ref, us2_ref,
                         wb1_ref, wb2_ref, cat1_ref, cat2_ref, carry_ref, *, tiles_per_seq, n_sub):
    j = pl.program_id(0)
    i = pl.program_id(1)
    tm = h_ref.shape[0]

    @pl.when((i == 0) & (j == 0))
    def _():
        carry_ref[...] = jnp.zeros_like(carry_ref)

    @pl.when(i == 0)
    def _():
        wb1_ref[...] = w1_ref[...].astype(BF16)
        wb2_ref[...] = w2_ref[...].astype(BF16)
        hs = hs_ref[...]
        us1_ref[...] = jnp.dot(hs, wb1_ref[...], preferred_element_type=F32)
        us2_ref[...] = jnp.dot(hs, wb2_ref[...], preferred_element_type=F32)

    first = (i % tiles_per_seq) == 0
    cat1_ref[0:SUBLANES, :] = jnp.where(first, s1_ref[...], carry_ref[0])
    cat2_ref[0:SUBLANES, :] = jnp.where(first, s2_ref[...], carry_ref[1])
    sub = tm // n_sub

    def project(r0):
        a = h_ref[r0:r0 + sub, :]
        cat1_ref[SUBLANES + r0:SUBLANES + r0 + sub, :] = jnp.dot(a, wb1_ref[...], preferred_element_type=F32)
        cat2_ref[SUBLANES + r0:SUBLANES + r0 + sub, :] = jnp.dot(a, wb2_ref[...], preferred_element_type=F32)

    def conv(r0, cw_ref, cb_ref, cat_ref):
        w = cw_ref[...]
        y = cat_ref[pl.ds(SUBLANES - 2 + r0, sub), :] * w[0:1, :] + cb_ref[...]
        y = y + cat_ref[pl.ds(SUBLANES - 1 + r0, sub), :] * w[1:2, :]
        return y + cat_ref[pl.ds(SUBLANES + r0, sub), :] * w[2:3, :]

    def gate(r0):
        y1 = conv(r0, cw1_ref, cb1_ref, cat1_ref)
        y2 = conv(r0, cw2_ref, cb2_ref, cat2_ref)
        g_ref[r0:r0 + sub, :] = (_silu(y1) * y2).astype(g_ref.dtype)

    project(0)
    for r in range(1, n_sub):
        project(r * sub)
        gate((r - 1) * sub)
    gate((n_sub - 1) * sub)
    tail1 = cat1_ref[tm:tm + SUBLANES, :]
    tail2 = cat2_ref[tm:tm + SUBLANES, :]
    carry_ref[0] = tail1
    carry_ref[1] = tail2
    t1_ref[...] = tail1
    t2_ref[...] = tail2


def ffn_up_fused(h, h_s, w_up, layer, state8, conv_w, conv_b, *, b, t, tm, tn, n_sub=1):
    m, k = h.shape
    ms = h_s.shape[0]
    c2 = w_up.shape[2]
    dff = c2 // 2
    nj = dff // tn
    assert t % tm == 0 and m == b * t and tm % SUBLANES == 0
    tps = t // tm
    wsp = lambda off: pl.BlockSpec((None, k, tn), lambda j, i: (layer, 0, j + off))
    ssp = lambda off: pl.BlockSpec((None, SUBLANES, tn), lambda j, i: (i // tps, 0, j + off))
    cwsp = lambda off: pl.BlockSpec((FFN_CONV, tn), lambda j, i: (0, j + off))
    cbsp = lambda off: pl.BlockSpec((1, tn), lambda j, i: (0, j + off))
    tail = pl.BlockSpec((None, SUBLANES, tn), lambda j, i: (i, 0, j))
    usp = pl.BlockSpec((ms, tn), lambda j, i: (0, j))
    cb = conv_b.reshape(1, c2)
    g, t1, t2, us1, us2 = pl.pallas_call(
        functools.partial(_ffn_up_fused_kernel, tiles_per_seq=tps, n_sub=n_sub),
        grid=(nj, m // tm),
        in_specs=[pl.BlockSpec((tm, k), lambda j, i: (i, 0)), pl.BlockSpec((ms, k), lambda j, i: (0, 0)),
                  wsp(0), wsp(nj), ssp(0), ssp(nj), cwsp(0), cwsp(nj), cbsp(0), cbsp(nj)],
        out_specs=[pl.BlockSpec((tm, tn), lambda j, i: (i, j)), tail, tail, usp, usp],
        out_shape=[jax.ShapeDtypeStruct((m, dff), BF16),
                   jax.ShapeDtypeStruct((m // tm, SUBLANES, dff), F32),
                   jax.ShapeDtypeStruct((m // tm, SUBLANES, dff), F32),
                   jax.ShapeDtypeStruct((ms, dff), F32),
                   jax.ShapeDtypeStruct((ms, dff), F32)],
        scratch_shapes=[pltpu.VMEM((k, tn), BF16),
                        pltpu.VMEM((k, tn), BF16),
                        pltpu.VMEM((SUBLANES + tm, tn), F32),
                        pltpu.VMEM((SUBLANES + tm, tn), F32),
                        pltpu.VMEM((2, SUBLANES, tn), F32)],
        compiler_params=_cparams(2),
        name="ffn_up_fused",
    )(h, h_s, w_up, w_up, state8, state8, conv_w, conv_w, cb, cb)
    tails = jnp.concatenate([t1[tps - 1::tps], t2[tps - 1::tps]], axis=-1)
    return g, tails, jnp.concatenate([us1, us2], axis=-1)


def _ssd_kernel(z_ref, x_ref, xh_ref, b_ref, bh_ref, c_ref, ch_ref, dt_ref,
                sx_ref, sb_ref, sc_ref, wx_ref, wb_ref, wc_ref, bx_ref, bb_ref, bc_ref,
                dtb_ref, alog_ref, dsk_ref, ng_ref, h0_ref,
                y_ref, hl_ref,
                st_ref, catx_ref, catb_ref, catc_ref, yacc_ref, *, t_valid, n_steps, cps):
    g = pl.program_id(1)
    c = pl.program_id(2)
    L = SSD_CHUNK
    rows_step = cps * L
    halo0 = SUBLANES - (SSM_CONV - 1)

    @pl.when(c == 0)
    def _():
        catx_ref[0:SUBLANES, :] = sx_ref[...]
        catb_ref[0:SUBLANES, :] = sb_ref[...]
        catc_ref[0:SUBLANES, :] = sc_ref[...]
        st_ref[...] = h0_ref[...].T

    @pl.when(c > 0)
    def _():
        catx_ref[0:SUBLANES, :] = xh_ref[...]
        catb_ref[0:SUBLANES, :] = bh_ref[...]
        catc_ref[0:SUBLANES, :] = ch_ref[...]

    catx_ref[SUBLANES:SUBLANES + rows_step, :] = x_ref[...]
    catb_ref[SUBLANES:SUBLANES + rows_step, :] = b_ref[...]
    catc_ref[SUBLANES:SUBLANES + rows_step, :] = c_ref[...]

    def conv_silu(cat_ref, w_ref, bias_ref, r0, sl):
        w = w_ref[:, sl]
        y = cat_ref[pl.ds(halo0 + r0, L), sl] * w[0:1, :] + bias_ref[:, sl]
        for k in range(1, SSM_CONV):
            y = y + cat_ref[pl.ds(halo0 + r0 + k, L), sl] * w[k:k + 1, :]
        return _silu(y)

    lanes0 = slice(0, LANES)
    row = lax.broadcasted_iota(jnp.int32, (L, LANES), 0)
    col = lax.broadcasted_iota(jnp.int32, (L, L), 1)
    causal = row >= col
    lane = lax.broadcasted_iota(jnp.int32, (1, LANES), 1)
    low = lane < SSM_HEAD_DIM
    a_neg = -jnp.exp(alog_ref[...])
    shift = (LANES - HEADS_PER_GROUP * g) % LANES

    for cc in range(cps):
        r0 = cc * L
        bm = conv_silu(catb_ref, wb_ref, bb_ref, r0, lanes0)
        cm = conv_silu(catc_ref, wc_ref, bc_ref, r0, lanes0)

        dtr = dt_ref[r0:r0 + L, :] + dtb_ref[...]
        dt = jnp.maximum(dtr, 0.0) + jnp.log(1.0 + jnp.exp(-jnp.abs(dtr)))
        dt = jnp.where(row + (c * rows_step + r0) < t_valid, dt, 0.0)
        ac = dt * a_neg
        s = 1
        while s < L:
            ac = ac + jnp.where(row >= s, pltpu.roll(ac, s, axis=0), 0.0)
            s *= 2
        ac_g = pltpu.roll(ac, shift, axis=1)
        dt_g = pltpu.roll(dt, shift, axis=1)
        ac_t = ac_g.T
        dt_t = dt_g.T

        bmb = bm.astype(BF16)
        cmb = cm.astype(BF16)
        cb = lax.dot_general(cmb, bmb, (((1,), (1,)), ((), ())), preferred_element_type=F32)
        bt = bm.T

        ssq = jnp.zeros((L, LANES), F32)
        for j in range(HEADS_PER_GROUP // 2):
            sl = slice(j * LANES, (j + 1) * LANES)
            xp = conv_silu(catx_ref, wx_ref, bx_ref, r0, sl)
            xpb = xp.astype(BF16)
            st_prev = st_ref[:, sl]
            yoff = jnp.dot(cmb, st_prev.astype(BF16), preferred_element_type=F32)
            parts = []
            for e in range(2):
                h = 2 * j + e
                ac_row = ac_t[h:h + 1, :]
                dt_row = dt_t[h:h + 1, :]
                ac_col = jnp.broadcast_to(ac_g[:, h:h + 1], (L, LANES))
                ac_last = ac_col[L - 1:L, :]
                decay = jnp.exp(jnp.where(causal, ac_col - ac_row, NEG_BIG))
                mh = (cb * decay * dt_row).astype(BF16)
                ydiag = jnp.dot(mh, xpb, preferred_element_type=F32)
                w_row = dt_row * jnp.exp(ac_last - ac_row)
                btw = (bt * w_row).astype(BF16)
                stc = jnp.dot(btw, xpb, preferred_element_type=F32)
                parts.append((ydiag, stc, jnp.exp(ac_col), jnp.exp(ac_last)))
            ydiag = jnp.where(low, parts[0][0], parts[1][0])
            stc = jnp.where(low, parts[0][1], parts[1][1])
            ecol = jnp.where(low, parts[0][2], parts[1][2])
            dec = jnp.where(low, parts[0][3], parts[1][3])
            y = ydiag + ecol * yoff + xp * dsk_ref[:, sl]
            y = y * _silu(z_ref[r0:r0 + L, sl])
            ssq = ssq + y * y
            yacc_ref[:, sl] = y
            st_ref[:, sl] = dec * st_prev + stc

        ms = jnp.sum(ssq, axis=-1, keepdims=True) * (1.0 / GROUP_WIDTH)
        rinv = lax.rsqrt(ms + NORM_EPS)
        y_ref[r0:r0 + L, :] = (yacc_ref[...] * rinv * ng_ref[...]).astype(y_ref.dtype)

    @pl.when(c == n_steps - 1)
    def _():
        hl_ref[...] = st_ref[...].T


def ssd_mixer(zx, dtraw, conv8, h0, conv_w, conv_b, dt_bias, a_log, d_skip, norm_g, *, t_valid):
    b, tp, _ = zx.shape
    nc = tp // SSD_CHUNK
    cps = next(d for d in (SSD_CHUNKS_PER_STEP, 2, 1) if nc % d == 0)
    ns = nc // cps
    L, gw, n = cps * SSD_CHUNK, GROUP_WIDTH, D_STATE
    hb = L // SUBLANES
    xoff = D_INNER // gw
    boff = (2 * D_INNER) // n
    coff = boff + N_SSM_GROUPS
    sboff = D_INNER // n
    scoff = sboff + N_SSM_GROUPS
    prev = lambda ci: jnp.maximum(ci * hb - 1, 0)
    in_specs = [
        pl.BlockSpec((None, L, gw), lambda bi, gi, ci: (bi, ci, gi)),
        pl.BlockSpec((None, L, gw), lambda bi, gi, ci: (bi, ci, xoff + gi)),
        pl.BlockSpec((None, SUBLANES, gw), lambda bi, gi, ci: (bi, prev(ci), xoff + gi)),
        pl.BlockSpec((None, L, n), lambda bi, gi, ci: (bi, ci, boff + gi)),
        pl.BlockSpec((None, SUBLANES, n), lambda bi, gi, ci: (bi, prev(ci), boff + gi)),
        pl.BlockSpec((None, L, n), lambda bi, gi, ci: (bi, ci, coff + gi)),
        pl.BlockSpec((None, SUBLANES, n), lambda bi, gi, ci: (bi, prev(ci), coff + gi)),
        pl.BlockSpec((None, L, N_SSM_HEADS), lambda bi, gi, ci: (bi, ci, 0)),
        pl.BlockSpec((None, SUBLANES, gw), lambda bi, gi, ci: (bi, 0, gi)),
        pl.BlockSpec((None, SUBLANES, n), lambda bi, gi, ci: (bi, 0, sboff + gi)),
        pl.BlockSpec((None, SUBLANES, n), lambda bi, gi, ci: (bi, 0, scoff + gi)),
        pl.BlockSpec((SSM_CONV, gw), lambda bi, gi, ci: (0, gi)),
        pl.BlockSpec((SSM_CONV, n), lambda bi, gi, ci: (0, sboff + gi)),
        pl.BlockSpec((SSM_CONV, n), lambda bi, gi, ci: (0, scoff + gi)),
        pl.BlockSpec((1, gw), lambda bi, gi, ci: (0, gi)),
        pl.BlockSpec((1, n), lambda bi, gi, ci: (0, sboff + gi)),
        pl.BlockSpec((1, n), lambda bi, gi, ci: (0, scoff + gi)),
        pl.BlockSpec((1, N_SSM_HEADS), lambda bi, gi, ci: (0, 0)),
        pl.BlockSpec((1, N_SSM_HEADS), lambda bi, gi, ci: (0, 0)),
        pl.BlockSpec((1, gw), lambda bi, gi, ci: (0, gi)),
        pl.BlockSpec((1, gw), lambda bi, gi, ci: (0, gi)),
        pl.BlockSpec((None, None, gw, n), lambda bi, gi, ci: (bi, gi, 0, 0)),
    ]
    d_rep = jnp.repeat(d_skip, SSM_HEAD_DIM).reshape(1, D_INNER)
    cb2 = conv_b.reshape(1, CONV_DIM)
    y, hl = pl.pallas_call(
        functools.partial(_ssd_kernel, t_valid=t_valid, n_steps=ns, cps=cps),
        grid=(b, N_SSM_GROUPS, ns),
        in_specs=in_specs,
        out_specs=[pl.BlockSpec((None, L, gw), lambda bi, gi, ci: (bi, ci, gi)),
                   pl.BlockSpec((None, None, gw, n), lambda bi, gi, ci: (bi, gi, 0, 0))],
        out_shape=[jax.ShapeDtypeStruct((b, tp, D_INNER), BF16),
                   jax.ShapeDtypeStruct((b, N_SSM_GROUPS, gw, n), F32)],
        scratch_shapes=[pltpu.VMEM((n, gw), F32),
                        pltpu.VMEM((SUBLANES + L, gw), F32),
                        pltpu.VMEM((SUBLANES + L, n), F32),
                        pltpu.VMEM((SUBLANES + L, n), F32),
                        pltpu.VMEM((SSD_CHUNK, gw), F32)],
        compiler_params=_cparams(3),
        name="ssd_mixer",
    )(zx, zx, zx, zx, zx, zx, zx, dtraw, conv8, conv8, conv8, conv_w, conv_w, conv_w,
      cb2, cb2, cb2, dt_bias.reshape(1, -1), a_log.reshape(1, -1), d_rep,
      norm_g.reshape(1, D_INNER), h0.reshape(b, N_SSM_GROUPS, gw, n))
    return y, hl.reshape(b, N_SSM_HEADS, SSM_HEAD_DIM, n)


def _rel_bucket(dist):
    max_exact = N_BUCKETS // 2
    nn = np.asarray(dist, dtype=np.int64)
    ratio = np.log(np.maximum(nn, 1) / max_exact) / np.log(REL_MAX_DIST / max_exact)
    large = np.minimum(max_exact + (np.maximum(ratio, 0.0) * (N_BUCKETS - max_exact)).astype(np.int64),
                       N_BUCKETS - 1)
    return np.where(nn < max_exact, nn, large).astype(np.int32)


def _bias_by_step(rel_bias, g):
    dists = np.arange(Q_BLOCK + 1) * DIL_RATES[g]
    return rel_bias[_rel_bucket(dists)][:, g * N_SLOTS:(g + 1) * N_SLOTS].astype(F32)


def _toeplitz_bias(rel_bias):
    ql = lax.broadcasted_iota(jnp.int32, (Q_BLOCK, 2 * Q_BLOCK), 0)
    kl = lax.broadcasted_iota(jnp.int32, (Q_BLOCK, 2 * Q_BLOCK), 1)
    step = ql + Q_BLOCK - kl
    ok = (step >= 0) & (step <= Q_BLOCK)
    onehot = (step[:, :, None] == jnp.arange(Q_BLOCK + 1)[None, None, :]).astype(F32)
    tabs = jnp.stack([_bias_by_step(rel_bias, g) for g in range(N_DIL_GROUPS)])
    tiles = jnp.einsum("qkx,gxs->gsqk", onehot, tabs, precision=lax.Precision.HIGHEST)
    return jnp.where(ok[None, None], tiles, NEG_BIG)


def _attn_prompt_kernel(q0_ref, q1_ref, q2_ref, k_ref, v_ref, bias_ref, o_ref,
                        qf_ref, acc_ref, m_ref, l_ref, *, t):
    q_refs = (q0_ref, q1_ref, q2_ref)
    order = sorted(range(N_DIL_GROUPS), key=lambda gi: -DIL_RATES[gi])
    for g in order:
        rate = DIL_RATES[g]
        span = rate * Q_BLOCK
        nqb = t // span
        for r in range(KV_REP):
            qf_ref[r] = q_refs[g][:, r * HEAD_DIM:(r + 1) * HEAD_DIM].astype(F32)

        def rows_at(start, rate=rate):
            return pl.ds(start, Q_BLOCK) if rate == 1 else pl.ds(start, Q_BLOCK, stride=rate)

        def body(idx, carry, first, g=g, span=span, nqb=nqb, rows_at=rows_at):
            if first:
                rho, qb = idx, 0
            else:
                rho = idx // (nqb - 1)
                qb = 1 + idx - rho * (nqb - 1)
            start = rho + span * qb
            if span == Q_BLOCK:
                start = pl.multiple_of(start, Q_BLOCK)
            rows = rows_at(start)
            qs = jnp.concatenate([qf_ref[r, rows, :] for r in range(KV_REP)], axis=0).astype(BF16)
            if first:
                k2 = k_ref[rows, :].astype(BF16)
                v2 = v_ref[rows, :].astype(BF16)
                bias = bias_ref[g, :, :, Q_BLOCK:].reshape(KV_REP * Q_BLOCK, Q_BLOCK)
            else:
                start_prev = start - span
                if span == Q_BLOCK:
                    start_prev = pl.multiple_of(start_prev, Q_BLOCK)
                rows_prev = rows_at(start_prev)
                k2 = jnp.concatenate([k_ref[rows_prev, :], k_ref[rows, :]], axis=0).astype(BF16)
                v2 = jnp.concatenate([v_ref[rows_prev, :], v_ref[rows, :]], axis=0).astype(BF16)
                bias = bias_ref[g].reshape(KV_REP * Q_BLOCK, 2 * Q_BLOCK)
            s = lax.dot_general(qs, k2, (((1,), (1,)), ((), ())), preferred_element_type=F32) + bias
            mg = jnp.max(s, axis=-1, keepdims=True)
            p = jnp.exp(s - mg)
            lg = jnp.sum(p, axis=-1, keepdims=True)
            og = jnp.dot(p.astype(BF16), v2, preferred_element_type=F32)
            for r in range(KV_REP):
                rs = slice(r * Q_BLOCK, (r + 1) * Q_BLOCK)
                o_r = og[rs]
                m_r = jnp.broadcast_to(mg[rs], (Q_BLOCK, HEAD_DIM))
                l_r = jnp.broadcast_to(lg[rs], (Q_BLOCK, HEAD_DIM))
                if g != order[0]:
                    m_old = m_ref[r, rows, :]
                    m_new = jnp.maximum(m_old, m_r)
                    a_old = jnp.exp(m_old - m_new)
                    a_new = jnp.exp(m_r - m_new)
                    o_r = acc_ref[r, rows, :] * a_old + o_r * a_new
                    l_r = l_ref[r, rows, :] * a_old + l_r * a_new
                    m_r = m_new
                if g == order[-1]:
                    acc_ref[r, rows, :] = o_r / l_r
                else:
                    acc_ref[r, rows, :] = o_r
                    m_ref[r, rows, :] = m_r
                    l_ref[r, rows, :] = l_r
            return carry

        lax.fori_loop(0, rate, functools.partial(body, first=True), 0, unroll=min(rate, 4))
        if nqb > 1:
            n_rest = rate * (nqb - 1)
            lax.fori_loop(0, n_rest, functools.partial(body, first=False), 0,
                          unroll=next(u for u in (4, 3, 2, 1) if n_rest % u == 0))
    for r in range(KV_REP):
        o_ref[:, r * HEAD_DIM:(r + 1) * HEAD_DIM] = acc_ref[r].astype(o_ref.dtype)


def attention_prompt(q, k, v, bias_t, b, t):
    rw = KV_REP * HEAD_DIM
    qspec = lambda g: pl.BlockSpec((t, rw), lambda bi, hi: (bi, g * N_KV_HEADS + hi))
    kvspec = pl.BlockSpec((None, t, HEAD_DIM), lambda bi, hi: (bi, 0, hi))
    return pl.pallas_call(
        functools.partial(_attn_prompt_kernel, t=t),
        grid=(b, N_KV_HEADS),
        in_specs=[qspec(0), qspec(1), qspec(2), kvspec, kvspec,
                  pl.BlockSpec((N_DIL_GROUPS, KV_REP, Q_BLOCK, 2 * Q_BLOCK), lambda bi, hi: (0, hi, 0, 0))],
        out_specs=pl.BlockSpec((t, rw), lambda bi, hi: (bi, hi)),
        out_shape=jax.ShapeDtypeStruct((b * t, N_SLOTS * HEAD_DIM), BF16),
        scratch_shapes=[pltpu.VMEM((KV_REP, t, HEAD_DIM), F32)] * 4,
        compiler_params=_cparams(2),
        name="attn_prompt",
    )(q, q, q, k, v, bias_t)


def _attn_sample_kernel(q_ref, kc_ref, kn_ref, vc_ref, vn_ref, bm_ref, bs_ref, o_ref,
                        kbuf_ref, vbuf_ref, *, t_new):
    w = WINDOW_MAX
    rows = N_DIL_GROUPS * t_new * KV_REP
    kbuf_ref[0:w, :] = kc_ref[...]
    kbuf_ref[w:w + SUBLANES, :] = kn_ref[...]
    vbuf_ref[0:w, :] = vc_ref[...]
    vbuf_ref[w:w + SUBLANES, :] = vn_ref[...]
    q = q_ref[...]
    qf = q.astype(F32)
    ridx = lax.broadcasted_iota(jnp.int32, (rows, 1), 0)

    def row_mask(g, j):
        base = (g * t_new + j) * KV_REP
        return (ridx >= base) & (ridx < base + KV_REP)

    kself = jnp.zeros((rows, HEAD_DIM), F32)
    vself = jnp.zeros((rows, HEAD_DIM), F32)
    for j in range(t_new):
        tok_j = row_mask(0, j)
        for g in range(1, N_DIL_GROUPS):
            tok_j = tok_j | row_mask(g, j)
        kself = jnp.where(tok_j, kn_ref[j:j + 1, :].astype(BF16).astype(F32), kself)
        vself = jnp.where(tok_j, vn_ref[j:j + 1, :], vself)
    s_self = jnp.sum(qf * kself, axis=-1, keepdims=True) + bs_ref[:, 0:1]
    s = jnp.zeros((rows, Q_BLOCK), F32)
    for g in range(N_DIL_GROUPS):
        for j in range(t_new):
            start = w + j - Q_BLOCK * DIL_RATES[g]
            ks = kbuf_ref[pl.ds(start, Q_BLOCK, stride=DIL_RATES[g]), :].astype(BF16)
            sg = lax.dot_general(q, ks, (((1,), (1,)), ((), ())), preferred_element_type=F32)
            s = jnp.where(row_mask(g, j), sg, s)
    s = s + bm_ref[...]
    m = jnp.maximum(jnp.max(s, axis=-1, keepdims=True), s_self)
    p = jnp.exp(s - m)
    p_self = jnp.exp(s_self - m)
    l = jnp.sum(p, axis=-1, keepdims=True) + p_self
    pb = p.astype(BF16)
    acc = p_self * vself
    for g in range(N_DIL_GROUPS):
        for j in range(t_new):
            start = w + j - Q_BLOCK * DIL_RATES[g]
            vs = vbuf_ref[pl.ds(start, Q_BLOCK, stride=DIL_RATES[g]), :].astype(BF16)
            og = jnp.dot(pb, vs, preferred_element_type=F32)
            acc = acc + jnp.where(row_mask(g, j), og, 0.0)
    o = acc / l
    lse = m + jnp.log(l)
    n = t_new * KV_REP
    ls = [lse[g * n:(g + 1) * n] for g in range(N_DIL_GROUPS)]
    mm = jnp.maximum(jnp.maximum(ls[0], ls[1]), ls[2])
    es = [jnp.exp(x - mm) for x in ls]
    inv = 1.0 / (es[0] + es[1] + es[2])
    out = sum((es[g] * inv) * o[g * n:(g + 1) * n] for g in range(N_DIL_GROUPS))
    o_ref[...] = out.astype(o_ref.dtype)


def attention_sample(q, k_new, v_new, k_cache, v_cache, rel_bias, b, t_new):
    assert PAST_LEN >= WINDOW_MAX and t_new <= SUBLANES
    rows = N_DIL_GROUPS * t_new * KV_REP
    n = t_new * KV_REP
    qs = q.reshape(b, t_new, N_DIL_GROUPS, N_KV_HEADS, KV_REP, HEAD_DIM)
    qs = jnp.transpose(qs, (0, 3, 2, 1, 4, 5)).reshape(b, N_KV_HEADS, rows, HEAD_DIM)
    pad = ((0, 0), (0, SUBLANES - t_new), (0, 0))
    kn = jnp.pad(k_new, pad)
    vn = jnp.pad(v_new, pad)
    bm, bs = [], []
    for g in range(N_DIL_GROUPS):
        tab = _bias_by_step(rel_bias, g)
        tab = tab.reshape(Q_BLOCK + 1, N_KV_HEADS, KV_REP)
        main = jnp.transpose(tab[1:][::-1], (1, 2, 0))
        bm.append(jnp.broadcast_to(main[:, None], (N_KV_HEADS, t_new, KV_REP, Q_BLOCK)))
        bs.append(jnp.broadcast_to(tab[0][:, None, :, None], (N_KV_HEADS, t_new, KV_REP, LANES)))
    bm = jnp.stack(bm, axis=1).reshape(N_KV_HEADS, rows, Q_BLOCK)
    bs = jnp.stack(bs, axis=1).reshape(N_KV_HEADS, rows, LANES)
    cache = pl.BlockSpec((None, WINDOW_MAX, HEAD_DIM), lambda bi, hi: (bi, 0, hi))
    new = pl.BlockSpec((None, SUBLANES, HEAD_DIM), lambda bi, hi: (bi, 0, hi))
    o = pl.pallas_call(
        functools.partial(_attn_sample_kernel, t_new=t_new),
        grid=(b, N_KV_HEADS),
        in_specs=[pl.BlockSpec((None, None, rows, HEAD_DIM), lambda bi, hi: (bi, hi, 0, 0)),
                  cache, new, cache, new,
                  pl.BlockSpec((None, rows, Q_BLOCK), lambda bi, hi: (hi, 0, 0)),
                  pl.BlockSpec((None, rows, LANES), lambda bi, hi: (hi, 0, 0))],
        out_specs=pl.BlockSpec((None, None, n, HEAD_DIM), lambda bi, hi: (bi, hi, 0, 0)),
        out_shape=jax.ShapeDtypeStruct((b, N_KV_HEADS, n, HEAD_DIM), BF16),
        scratch_shapes=[pltpu.VMEM((WINDOW_MAX + SUBLANES, HEAD_DIM), F32),
                        pltpu.VMEM((WINDOW_MAX + SUBLANES, HEAD_DIM), F32)],
        compiler_params=_cparams(2),
        name="attn_sample",
    )(qs, k_cache, kn, v_cache, vn, bm, bs)
    o = o.reshape(b, N_KV_HEADS, t_new, KV_REP, HEAD_DIM)
    return jnp.transpose(o, (0, 2, 1, 3, 4)).reshape(b * t_new, N_SLOTS * HEAD_DIM)


def _pad_rows_front(state, rows):
    return jnp.pad(state, ((0, 0), (rows - state.shape[1], 0), (0, 0)))


TM_PROMPT = 1024
TN_WS = 512
TN_FFN_UP = 256
TM_LONGK = 512
TN_SSM_OUT = 512
TN_FFN_DOWN = 512
TM_NORM = 512
TC_GATE_SAMPLE = D_FF // 2


def _ssd_block(zx, dtraw, b, t, conv0, h0, layer, W):
    zx = zx.reshape(b, t, XBC_END)
    dtraw = dtraw.reshape(b, t, N_SSM_HEADS)
    conv_new = zx[:, t - (SSM_CONV - 1):, D_INNER:]
    tp = -(-t // SSD_CHUNK) * SSD_CHUNK
    if tp != t:
        zx = jnp.pad(zx, ((0, 0), (0, tp - t), (0, 0)))
        dtraw = jnp.pad(dtraw, ((0, 0), (0, tp - t), (0, 0)))
    y, hs = ssd_mixer(zx, dtraw, _pad_rows_front(conv0, SUBLANES), h0,
                      W["ssm_conv_w"][layer], W["ssm_conv_b"][layer], W["ssm_dt_bias"][layer],
                      W["ssm_a_log"][layer], W["ssm_d"][layer], W["ssm_norm"][layer], t_valid=t)
    return y[:, :t].reshape(b * t, D_INNER), hs, conv_new


def _trunk(xp, xs, pp, ps, bp, tp, bs, ts, ssm0_s, convs0_s, convf0_s, k_cache, v_cache, W):
    ms = bs * ts
    ssm_p, ssm_s, convs_p, convs_s, convf_p, convf_s = [], [], [], [], [], []
    kp = vp = ks = vs = None
    zero_ssm = jnp.zeros((bp,) + ssm0_s.shape[2:], ssm0_s.dtype)
    zero_convs = jnp.zeros((bp,) + convs0_s.shape[2:], convs0_s.dtype)
    zero_convf8 = jnp.zeros((bp, SUBLANES, 2 * D_FF), convf0_s.dtype)
    norm2 = lambda gain: (rmsnorm(xp, gain, TM_NORM), rmsnorm(xs, gain, ms))
    ws = functools.partial(ws_matmul, tm=TM_PROMPT)
    for i in range(DEPTH):
        if i < N_A_LAYERS:
            hp, hs = norm2(W["norm_mix"][i])
            zx_p, zx_s = ws(hp, hs, W["ssm_w_in"], i, n_cols=XBC_END, tn=TN_WS, name="ssm_in")
            dt_p, dt_s = ws(hp, hs, W["ssm_w_in"], i, n_cols=N_SSM_HEADS, col0=XBC_END, tn=N_SSM_HEADS,
                            name="ssm_in_dt")
            y_p, h_p, c_p = _ssd_block(zx_p, dt_p, bp, tp, zero_convs, zero_ssm, i, W)
            y_s, h_s, c_s = _ssd_block(zx_s, dt_s, bs, ts, convs0_s[i], ssm0_s[i], i, W)
            ssm_p.append(h_p); ssm_s.append(h_s); convs_p.append(c_p); convs_s.append(c_s)
            xp = mm_residual(y_p, W["ssm_w_out"], i, xp, tm=TM_LONGK, tn=TN_SSM_OUT, name="ssm_out")
            xs = mm_residual(y_s, W["ssm_w_out"], i, xs, tm=ms, tn=TN_SSM_OUT, name="ssm_out")
        else:
            li = i - N_A_LAYERS
            if li == 0:
                hp, hs = norm2(W["norm_kv"])
                kp, ks = ws(hp, hs, W["w_k"], 0, n_cols=N_KV_HEADS * HEAD_DIM, tn=TN_WS, kind="headnorm",
                            extras_p=(W["k_gain"],), name="k_proj")
                vp, vs = ws(hp, hs, W["w_v"], 0, n_cols=N_KV_HEADS * HEAD_DIM, tn=TN_WS, name="v_proj")
                kp, vp = (a.reshape(bp, tp, N_KV_HEADS * HEAD_DIM) for a in (kp, vp))
                ks, vs = (a.reshape(bs, ts, N_KV_HEADS * HEAD_DIM) for a in (ks, vs))
            hp, hs = norm2(W["norm_mix"][i])
            qp, qs = ws(hp, hs, W["attn_w_q"], li, n_cols=N_DIL_GROUPS * D_MODEL, tn=TN_WS, kind="headnorm",
                        extras_p=(W["q_gain"][li],), out_dtype=BF16, name="q_proj")
            op = attention_prompt(qp, kp, vp, W["bias_tiles"], bp, tp)
            os_ = attention_sample(qs, ks, vs, k_cache, v_cache, W["rel_bias"], bs, ts)
            xp, xs = ws(op, os_, W["attn_w_o"], li, n_cols=D_MODEL, tn=TN_WS, kind="residual",
                        extras_p=(xp,), extras_s=(xs,), name="attn_out")
        hp, hs = norm2(W["norm_ffn"][i])
        g_p, tails, u_s = ffn_up_fused(hp, hs, W["ffn_w_up"], i, zero_convf8, W["ffn_conv_w"][i],
                                       W["ffn_conv_b"][i], b=bp, t=tp, tm=TM_PROMPT, tn=TN_FFN_UP)
        convf_p.append(tails[:, SUBLANES - (FFN_CONV - 1):])
        u_s = u_s.reshape(bs, ts, 2 * D_FF)
        convf_s.append(u_s[:, ts - (FFN_CONV - 1):])
        tpad = -(-ts // SUBLANES) * SUBLANES
        u_s = jnp.pad(u_s, ((0, 0), (0, tpad - ts), (0, 0)))
        g_s = ffn_gate(u_s, _pad_rows_front(convf0_s[i], SUBLANES), W["ffn_conv_w"][i], W["ffn_conv_b"][i],
                       tt=tpad, tc=TC_GATE_SAMPLE)[:, :ts].reshape(ms, D_FF)
        xp = mm_residual(g_p, W["ffn_w_down"], i, xp, tm=TM_LONGK, tn=TN_FFN_DOWN, name="ffn_down")
        xs = mm_residual(g_s, W["ffn_w_down"], i, xs, tm=ms, tn=TN_FFN_DOWN, name="ffn_down")
        hp, hs = norm2(W["norm_ple"][i])
        xp, xs = ws(hp, hs, W["ple_w_gate"], i, n_cols=D_MODEL, tn=TN_WS, kind="ple",
                    extras_p=(pp[i], xp), extras_s=(ps[i], xs), aux=W["ple_w_proj"], name="ple_gate")
    st = jnp.stack
    return (xp, xs, st(ssm_p), st(ssm_s), st(convs_p), st(convs_s), st(convf_p), st(convf_s), kp, vp, ks, vs)


def kernel(x_prompt, x_sample, p_prompt, p_sample, state_ssm, state_conv_ssm, state_conv_ffn, cache_k_window, cache_v_window, norm_mix, norm_ffn, norm_ple, ssm_w_in, ssm_conv_w, ssm_conv_b, ssm_dt_bias, ssm_a_log, ssm_d, ssm_norm, ssm_w_out, norm_kv, w_k, w_v, k_norm, attn_w_q, q_norm, attn_w_o, rel_bias, ffn_w_up, ffn_conv_w, ffn_conv_b, ffn_w_down, ple_w_proj, ple_w_gate):
    bp, tp, _ = x_prompt.shape
    bs, ts, _ = x_sample.shape
    assert tp % (Q_BLOCK * max(DIL_RATES)) == 0 and tp <= WINDOW_MAX
    assert cache_k_window.shape[1] == WINDOW_MAX
    q_gain = jnp.broadcast_to(q_norm[:, :, None, :], (DEPTH - N_A_LAYERS, N_DIL_GROUPS, N_SLOTS, HEAD_DIM))
    W = dict(
        norm_mix=norm_mix, norm_ffn=norm_ffn, norm_ple=norm_ple, norm_kv=norm_kv,
        ssm_w_in=ssm_w_in, ssm_conv_w=ssm_conv_w, ssm_conv_b=ssm_conv_b, ssm_dt_bias=ssm_dt_bias,
        ssm_a_log=ssm_a_log, ssm_d=ssm_d, ssm_norm=ssm_norm, ssm_w_out=ssm_w_out.astype(BF16),
        w_k=w_k[None], w_v=w_v[None], k_gain=jnp.tile(k_norm, N_KV_HEADS).reshape(1, -1),
        attn_w_q=attn_w_q, q_gain=(q_gain * (HEAD_DIM ** -0.5)).reshape(DEPTH - N_A_LAYERS, 1, -1),
        attn_w_o=attn_w_o, rel_bias=rel_bias, bias_tiles=_toeplitz_bias(rel_bias),
        ffn_w_up=ffn_w_up, ffn_conv_w=ffn_conv_w, ffn_conv_b=ffn_conv_b,
        ffn_w_down=ffn_w_down.astype(BF16), ple_w_proj=ple_w_proj, ple_w_gate=ple_w_gate)

    kc = cache_k_window.reshape(bs, WINDOW_MAX, N_KV_HEADS * HEAD_DIM)
    vc = cache_v_window.reshape(bs, WINDOW_MAX, N_KV_HEADS * HEAD_DIM)
    (y_p, y_s, ssm_p, ssm_s, convs_p, convs_s, convf_p, convf_s, k_p, v_p, k_s, v_s) = _trunk(
        x_prompt.reshape(bp * tp, D_MODEL), x_sample.reshape(bs * ts, D_MODEL),
        p_prompt.reshape(DEPTH, bp * tp, PLE_DIM).astype(BF16),
        p_sample.reshape(DEPTH, bs * ts, PLE_DIM).astype(BF16),
        bp, tp, bs, ts, state_ssm, state_conv_ssm, state_conv_ffn, kc, vc, W)
    kv4 = lambda a, b_, t_: a.reshape(b_, t_, N_KV_HEADS, HEAD_DIM)
    return (y_p.reshape(bp, tp, D_MODEL), y_s.reshape(bs, ts, D_MODEL), ssm_p, ssm_s, convs_p, convs_s,
            convf_p, convf_s, kv4(k_p, bp, tp), kv4(v_p, bp, tp), kv4(k_s, bs, ts), kv4(v_s, bs, ts))
```
